```python
import jax
import jax.numpy as jnp
from jax import lax
import numpy as np

D_MODEL = 2048
BATCH = 16
SEQ = 256
DEPTH = 4
DEC_BATCH = 2
DEC_SEQ = 2048
PAST_LEN = 256

GRID_W = 64
D_RG = 1024
RG_BLOCKS = 16
RG_BLOCK_W = D_RG // RG_BLOCKS
RG_CONV_W = 4
RG_C = 8.0
D_ML = D_MODEL - D_RG
ML_HEADS = 4
ML_HD = D_ML // ML_HEADS
ML_CHUNK = 64
N_EXPERTS = 32
TOP_K = 4
D_EXPERT = 2048
SWIGLU_LIMIT = 7.0
SWIGLU_ALPHA = 1.702
MOE_BLOCK = 128
LN_EPS = 1e-5
DN_ALPHA = (2 * DEPTH) ** 0.25
DN_BETA = (8 * DEPTH) ** -0.25
D_IN = 2 * D_RG + 4 * D_ML + 4 * ML_HEADS
F32 = jnp.float32

kernel_name = 'hymba_rglru_mlstm_moe_diffusion'


def layer_norm(x, g, b):
    x32 = x.astype(F32)
    mu = jnp.mean(x32, -1, keepdims=True)
    var = jnp.mean(jnp.square(x32 - mu), -1, keepdims=True)
    return ((x32 - mu) * lax.rsqrt(var + LN_EPS) * g.astype(F32) + b.astype(F32)).astype(x.dtype)


def short_conv(x, w, b, grid):
    B, T, C = x.shape
    xs = x.reshape(B * (T // GRID_W), GRID_W, C) if grid else x
    L = xs.shape[1]
    left = RG_CONV_W // 2
    xp = jnp.pad(xs, ((0, 0), (left, RG_CONV_W - 1 - left), (0, 0)))
    y = sum(xp[:, j:j + L] * w[j] for j in range(RG_CONV_W)) + b
    return y.reshape(B, T, C)


def block_diag(x, w, b):
    B, T, C = x.shape
    xb = x.reshape(B, T, RG_BLOCKS, RG_BLOCK_W)
    return jnp.einsum('btnj,njk->btnk', xb, w).reshape(B, T, C) + b


def _lin_combine(left, right):
    a_l, b_l = left
    a_r, b_r = right
    return a_l * a_r, a_r * b_l + b_r


def rglru_scan(x, wa, ba, wx, bx, lam, h0):
    r = jax.nn.sigmoid(block_diag(x, wa, ba))
    i = jax.nn.sigmoid(block_diag(x, wx, bx))
    log_a = -RG_C * r * jax.nn.softplus(-lam.astype(F32))
    a = jnp.exp(log_a)
    u = jnp.sqrt(-jnp.expm1(2.0 * log_a)) * (i * x)
    a_cum, u_cum = lax.associative_scan(_lin_combine, (a, u), axis=1)
    h = a_cum * h0[:, None, :] + u_cum
    return h, h[:, -1]


def mlstm_scan(q, k, v, ig, lf, C0, n0, m0):
    B, H, T, hd = q.shape
    nc = T // ML_CHUNK

    def chunks(a):
        return jnp.moveaxis(a.reshape(B, H, nc, ML_CHUNK, *a.shape[3:]), 2, 0)

    causal = jnp.tril(jnp.ones((ML_CHUNK, ML_CHUNK), bool))

    def step(carry, xs):
        C, n, m = carry
        qc, kc, vc, ic, fc = xs
        b = jnp.cumsum(fc, -1)
        dmat = jnp.where(causal, b[..., :, None] - b[..., None, :] + ic[..., None, :], -jnp.inf)
        inter = b + m[..., None]
        m_t = jnp.maximum(inter, jnp.max(dmat, -1))
        w_inter = jnp.exp(inter - m_t)
        s = jnp.einsum('bhik,bhjk->bhij', qc, kc) * jnp.exp(dmat - m_t[..., None])
        num = w_inter[..., None] * jnp.einsum('bhik,bhkv->bhiv', qc, C) + jnp.einsum('bhij,bhjv->bhiv', s, vc)
        den = w_inter * jnp.einsum('bhik,bhk->bhi', qc, n) + jnp.sum(s, -1)
        h = num / jnp.maximum(jnp.abs(den), jnp.exp(-m_t))[..., None]
        b_last = b[..., -1]
        g = b_last[..., None] - b + ic
        m_new = jnp.maximum(b_last + m, jnp.max(g, -1))
        decay = jnp.exp(b_last + m - m_new)
        wk = jnp.exp(g - m_new[..., None])
        C_new = decay[..., None, None] * C + jnp.einsum('bhj,bhjk,bhjv->bhkv', wk, kc, vc)
        n_new = decay[..., None] * n + jnp.einsum('bhj,bhjk->bhk', wk, kc)
        return (C_new, n_new, m_new), h

    (C, n, m), hs = lax.scan(step, (C0, n0, m0), (chunks(q), chunks(k), chunks(v), chunks(ig), chunks(lf)))
    return jnp.moveaxis(hs, 0, 2).reshape(B, H, T, hd), (C, n, m)


def mixer(u, w_in, ml_gate_b, conv_w, conv_b, wa, ba, wx, bx, lam, norm_w, w_out, grid, rg_h0, C0, n0, m0):
    B, T, _ = u.shape
    proj = u @ w_in
    rx, rgate, q, k, v, og, gates = jnp.split(
        proj, [D_RG, 2 * D_RG, 2 * D_RG + D_ML, 2 * D_RG + 2 * D_ML, 2 * D_RG + 3 * D_ML, 2 * D_RG + 4 * D_ML], axis=-1)
    xc = short_conv(rx.astype(F32), conv_w, conv_b, grid)
    rg_h0 = rg_h0.astype(F32)
    h_f, hT_f = rglru_scan(xc, wa[0], ba[0], wx[0], bx[0], lam[0], rg_h0[:, 0])
    h_b, hT_b = rglru_scan(jnp.flip(xc, 1), wa[1], ba[1], wx[1], bx[1], lam[1], rg_h0[:, 1])
    rg_y = (h_f + jnp.flip(h_b, 1)) * jax.nn.gelu(rgate.astype(F32))
    def heads(a):
        return a.astype(F32).reshape(B, T, ML_HEADS, ML_HD).transpose(0, 2, 1, 3)
    qh, kh, vh = heads(q), heads(k) * ML_HD ** -0.5, heads(v)
    g = (gates.astype(F32) + ml_gate_b).reshape(B, T, 4, ML_HEADS).transpose(2, 0, 3, 1)
    C0, n0, m0 = C0.astype(F32), n0.astype(F32), m0.astype(F32)
    hm_f, (C_f, n_f, m_f) = mlstm_scan(qh, kh, vh, g[0], jax.nn.log_sigmoid(g[1]), C0[:, 0], n0[:, 0], m0[:, 0])
    flip_t = lambda a: jnp.flip(a, 2)
    hm_b, (C_b, n_b, m_b) = mlstm_scan(flip_t(qh), flip_t(kh), flip_t(vh), flip_t(g[2]),
                                       flip_t(jax.nn.log_sigmoid(g[3])), C0[:, 1], n0[:, 1], m0[:, 1])
    hm = hm_f + flip_t(hm_b)
    mu = jnp.mean(hm, -1, keepdims=True)
    var = jnp.mean(jnp.square(hm - mu), -1, keepdims=True)
    hn = ((hm - mu) * lax.rsqrt(var + LN_EPS)).transpose(0, 2, 1, 3).reshape(B, T, D_ML) * norm_w
    ml_y = jax.nn.sigmoid(og.astype(F32)) * hn
    out = jnp.concatenate([rg_y, ml_y], -1).astype(u.dtype) @ w_out
    states = (jnp.stack([hT_f, hT_b], 1), jnp.stack([C_f, C_b], 1), jnp.stack([n_f, n_b], 1), jnp.stack([m_f, m_b], 1))
    return out, states


def moe(u, router_w, router_b, w1, b1, w2, b2):
    B, T, D = u.shape
    N = B * T
    x = u.reshape(N, D)
    logits = (x @ router_w + router_b).astype(F32)
    top_v, top_i = lax.top_k(logits, TOP_K)
    gates = jax.nn.softmax(top_v, axis=-1)
    A = N * TOP_K
    e_flat = top_i.reshape(A)
    tok = jnp.arange(A, dtype=jnp.int32) // TOP_K
    order = jnp.argsort(e_flat)
    e_s = e_flat[order]
    tok_s = tok[order]
    w_s = gates.reshape(A)[order]
    counts = jnp.zeros((N_EXPERTS,), jnp.int32).at[e_flat].add(1)
    starts = jnp.cumsum(counts) - counts
    padded = (counts + MOE_BLOCK - 1) // MOE_BLOCK * MOE_BLOCK
    pad_ends = jnp.cumsum(padded)
    pad_starts = pad_ends - padded
    dest = pad_starts[e_s] + jnp.arange(A, dtype=jnp.int32) - starts[e_s]
    n_blocks = -(-A // MOE_BLOCK) + N_EXPERTS
    R = n_blocks * MOE_BLOCK
    tok_buf = jnp.full((R,), N, jnp.int32).at[dest].set(tok_s)
    w_buf = jnp.zeros((R,), F32).at[dest].set(w_s)
    blk_e = jnp.minimum(jnp.searchsorted(pad_ends, jnp.arange(n_blocks, dtype=jnp.int32) * MOE_BLOCK, side='right'),
                        N_EXPERTS - 1)
    x_buf = jnp.concatenate([x, jnp.zeros((1, D), x.dtype)], 0)[tok_buf].reshape(n_blocks, MOE_BLOCK, D)

    def expert_block(args):
        xe, e = args
        h = xe @ w1[e] + b1[e]
        gl, lin = jnp.split(h, 2, axis=-1)
        gl = jnp.minimum(gl, SWIGLU_LIMIT)
        lin = jnp.clip(lin, -SWIGLU_LIMIT, SWIGLU_LIMIT)
        return (gl * jax.nn.sigmoid(SWIGLU_ALPHA * gl) * (lin + 1)) @ w2[e] + b2[e]

    y_buf = lax.map(expert_block, (x_buf, blk_e)).reshape(R, D)
    y = jax.ops.segment_sum(y_buf * w_buf[:, None].astype(y_buf.dtype), tok_buf, num_segments=N + 1)[:N]
    return y.reshape(B, T, D)


def setup_inputs(seed: int = 0) -> dict:
    key = jax.random.key(seed)
    ks = jax.random.split(key, 32)

    def nrm(k, shape, scale):
        return jax.random.normal(k, shape, F32) * scale

    f_bias = jnp.linspace(3.0, 6.0, ML_HEADS, dtype=F32)
    zero_h = jnp.zeros((ML_HEADS,), F32)
    gate_base = jnp.concatenate([zero_h, f_bias, zero_h, f_bias])
    a0 = jax.random.uniform(ks[18], (DEPTH, 2, D_RG), F32, 0.9, 0.999)
    a_base = a0 ** (1.0 / RG_C)
    lam = jnp.log(a_base) - jnp.log1p(-a_base)
    return {
        'x_prompt': nrm(ks[0], (BATCH, SEQ, D_MODEL), 1.0),
        'x_sample': nrm(ks[1], (DEC_BATCH, DEC_SEQ, D_MODEL), 1.0),
        'state_rglru': nrm(ks[2], (DEC_BATCH, DEPTH, 2, D_RG), 0.5),
        'state_mlstm_C': nrm(ks[3], (DEC_BATCH, DEPTH, 2, ML_HEADS, ML_HD, ML_HD), 0.1),
        'state_mlstm_n': nrm(ks[4], (DEC_BATCH, DEPTH, 2, ML_HEADS, ML_HD), 0.1),
        'state_mlstm_m': nrm(ks[5], (DEC_BATCH, DEPTH, 2, ML_HEADS), 0.5),
        'c': nrm(ks[6], (DEC_BATCH, D_MODEL), 1.0),
        'c_ctx': nrm(ks[7], (D_MODEL,), 1.0),
        'w_ada': nrm(ks[8], (DEPTH, D_MODEL, 6 * D_MODEL), 0.5 * D_MODEL ** -0.5),
        'b_ada': nrm(ks[9], (DEPTH, 6 * D_MODEL), 0.02),
        'w_in': nrm(ks[10], (DEPTH, D_MODEL, D_IN), D_MODEL ** -0.5),
        'ml_gate_b': gate_base + nrm(ks[11], (DEPTH, 4 * ML_HEADS), 0.1),
        'rg_conv_w': nrm(ks[12], (DEPTH, RG_CONV_W, D_RG), RG_CONV_W ** -0.5),
        'rg_conv_b': nrm(ks[13], (DEPTH, D_RG), 0.02),
        'rg_wa': nrm(ks[14], (DEPTH, 2, RG_BLOCKS, RG_BLOCK_W, RG_BLOCK_W), RG_BLOCK_W ** -0.5),
        'rg_ba': nrm(ks[15], (DEPTH, 2, D_RG), 0.02),
        'rg_wx': nrm(ks[16], (DEPTH, 2, RG_BLOCKS, RG_BLOCK_W, RG_BLOCK_W), RG_BLOCK_W ** -0.5),
        'rg_bx': nrm(ks[17], (DEPTH, 2, D_RG), 0.02),
        'rg_lambda': lam,
        'ml_norm_w': 1.0 + nrm(ks[19], (DEPTH, D_ML), 0.02),
        'w_out': nrm(ks[20], (DEPTH, D_MODEL, D_MODEL), DN_BETA * D_MODEL ** -0.5),
        'ln1_g': 1.0 + nrm(ks[21], (DEPTH, D_MODEL), 0.02),
        'ln1_b': nrm(ks[22], (DEPTH, D_MODEL), 0.02),
        'router_w': nrm(ks[23], (DEPTH, D_MODEL, N_EXPERTS), D_MODEL ** -0.5),
        'router_b': nrm(ks[24], (DEPTH, N_EXPERTS), 0.01),
        'moe_w1': nrm(ks[25], (DEPTH, N_EXPERTS, D_MODEL, 2 * D_EXPERT), D_MODEL ** -0.5),
        'moe_b1': nrm(ks[26], (DEPTH, N_EXPERTS, 2 * D_EXPERT), 0.02),
        'moe_w2': nrm(ks[27], (DEPTH, N_EXPERTS, D_EXPERT, D_MODEL), DN_BETA * D_EXPERT ** -0.5),
        'moe_b2': nrm(ks[28], (DEPTH, N_EXPERTS, D_MODEL), 0.02),
        'ln2_g': 1.0 + nrm(ks[29], (DEPTH, D_MODEL), 0.02),
        'ln2_b': nrm(ks[30], (DEPTH, D_MODEL), 0.02),
    }


def reference(x_prompt, x_sample, state_rglru, state_mlstm_C, state_mlstm_n, state_mlstm_m, c, c_ctx,
              w_ada, b_ada, w_in, ml_gate_b, rg_conv_w, rg_conv_b, rg_wa, rg_ba, rg_wx, rg_bx, rg_lambda,
              ml_norm_w, w_out, ln1_g, ln1_b, router_w, router_b, moe_w1, moe_b1, moe_w2, moe_b2, ln2_g, ln2_b):

    def run_layer(l, x, cond, grid, rg_h0, C0, n0, m0):
        mod = (jax.nn.silu(cond) @ w_ada[l] + b_ada[l])[:, None, :]
        sh1, sc1, g1, sh2, sc2, g2 = jnp.split(mod, 6, axis=-1)
        mix, st = mixer(x * (1 + sc1) + sh1, w_in[l], ml_gate_b[l], rg_conv_w[l], rg_conv_b[l],
                        rg_wa[l], rg_ba[l], rg_wx[l], rg_bx[l], rg_lambda[l], ml_norm_w[l], w_out[l],
                        grid, rg_h0, C0, n0, m0)
        x = layer_norm(DN_ALPHA * x + g1 * mix, ln1_g[l], ln1_b[l])
        ff = moe(x * (1 + sc2) + sh2, router_w[l], router_b[l], moe_w1[l], moe_b1[l], moe_w2[l], moe_b2[l])
        x = layer_norm(DN_ALPHA * x + g2 * ff, ln2_g[l], ln2_b[l])
        return x, st

    B = x_prompt.shape[0]
    z_rg = jnp.zeros((B, 2, D_RG), F32)
    z_C = jnp.zeros((B, 2, ML_HEADS, ML_HD, ML_HD), F32)
    z_n = jnp.zeros((B, 2, ML_HEADS, ML_HD), F32)
    z_m = jnp.zeros((B, 2, ML_HEADS), F32)
    xp = x_prompt
    sts = []
    for l in range(DEPTH):
        xp, st = run_layer(l, xp, c_ctx[None, :], False, z_rg, z_C, z_n, z_m)
        sts.append(st)
    new_rglru = jnp.stack([s[0] for s in sts], axis=1)
    new_mlstm_C = jnp.stack([s[1] for s in sts], axis=1)
    new_mlstm_n = jnp.stack([s[2] for s in sts], axis=1)
    new_mlstm_m = jnp.stack([s[3] for s in sts], axis=1)

    xs = x_sample
    for l in range(DEPTH):
        xs, _ = run_layer(l, xs, c, True, state_rglru[:, l], state_mlstm_C[:, l],
                          state_mlstm_n[:, l], state_mlstm_m[:, l])

    return (xp, xs, new_rglru, new_mlstm_C, new_mlstm_n, new_mlstm_m)
```

```python
import functools

import jax
import jax.numpy as jnp
from jax import lax
from jax.experimental import pallas as pl
from jax.experimental.pallas import tpu as pltpu

F32 = jnp.float32
BF16 = jnp.bfloat16

D_MODEL = 2048
D_RG = 1024
RG_BLOCK_W = 64
RG_CONV_W = 4
RG_C = 8.0
GRID_W = 64
D_ML = 1024
ML_HEADS = 4
ML_HD = 256
TOP_K = 4
SWIGLU_LIMIT = 7.0
SWIGLU_ALPHA = 1.702
LN_EPS = 1e-5
N_MOD = 6
CHUNK = 256
SUBLANES = 8
LANES = 128
N_GATES = 4 * ML_HEADS
COL_Q = 2 * D_RG
COL_K = COL_Q + D_ML
COL_V = COL_K + D_ML
COL_O = COL_V + D_ML
D_PROJ = COL_O + D_ML

MOE_CAP = 1280
MOE_RB = 256
MOE_TF = 256
VMEM_LIMIT = 56 * 1024 * 1024
HIGHEST = lax.Precision.HIGHEST


def _sigmoid(x):
    return 1.0 / (1.0 + jnp.exp(-x))


def _softplus(x):
    return jnp.maximum(x, 0.0) + jnp.log1p(jnp.exp(-jnp.abs(x)))


def _gelu_tanh(x):
    return 0.5 * x * (1.0 + jnp.tanh(0.7978845608028654 * (x + 0.044715 * (x * x * x))))


def _layer_norm(z, g, b):
    mu = jnp.mean(z, axis=-1, keepdims=True)
    zc = z - mu
    var = jnp.mean(zc * zc, axis=-1, keepdims=True)
    return zc * lax.rsqrt(var + LN_EPS) * g + b


def _params(*sem):
    return pltpu.CompilerParams(dimension_semantics=sem, vmem_limit_bytes=VMEM_LIMIT)


def _cond_index(row_start, n_p, ds):
    return jnp.where(row_start < n_p, 0, 1 + (row_start - n_p) // ds)


def _ada_body(c_ref, w_ref, b_ref, o_ref):
    c = c_ref[...]
    s = c * _sigmoid(c)
    o_ref[...] = jnp.dot(s, w_ref[...], preferred_element_type=F32, precision=HIGHEST) + b_ref[...]


def _ada(conds, w_ada, b_ada, tn=1024):
    depth, d, n6 = w_ada.shape
    nc = conds.shape[0]
    return pl.pallas_call(
        _ada_body,
        out_shape=jax.ShapeDtypeStruct((depth, nc, n6), F32),
        grid=(depth, n6 // tn),
        in_specs=[
            pl.BlockSpec((nc, d), lambda l, j: (0, 0)),
            pl.BlockSpec((None, d, tn), lambda l, j: (l, 0, j)),
            pl.BlockSpec((None, 1, tn), lambda l, j: (l, 0, j)),
        ],
        out_specs=pl.BlockSpec((None, nc, tn), lambda l, j: (l, 0, j)),
        compiler_params=_params("arbitrary", "arbitrary"),
        name="ada",
    )(conds, w_ada, b_ada.reshape(depth, 1, n6))


def _modulate_body(x_ref, mod_ref, u_ref):
    u_ref[...] = (x_ref[...] * (1.0 + mod_ref[1:2, :]) + mod_ref[0:1, :]).astype(BF16)


def _modulate(x, mod_l, n_p, ds, tm=512):
    n, d = x.shape
    return pl.pallas_call(
        _modulate_body,
        out_shape=jax.ShapeDtypeStruct((n, d), BF16),
        grid=(n // tm,),
        in_specs=[
            pl.BlockSpec((tm, d), lambda i: (i, 0)),
            pl.BlockSpec((None, N_MOD, d), lambda i: (_cond_index(i * tm, n_p, ds), 0, 0)),
        ],
        out_specs=pl.BlockSpec((tm, d), lambda i: (i, 0)),
        compiler_params=_params("arbitrary"),
        name="modulate",
    )(x, mod_l)


def _inproj_body(u_ref, w_ref, o_ref):
    o_ref[...] = jnp.dot(u_ref[...], w_ref[...], preferred_element_type=F32)


def _inproj(u, w_bf, tm=1024, tn=768):
    n, d = u.shape
    return pl.pallas_call(
        _inproj_body,
        out_shape=jax.ShapeDtypeStruct((n, D_PROJ), F32),
        grid=(D_PROJ // tn, n // tm),
        in_specs=[
            pl.BlockSpec((tm, d), lambda j, i: (i, 0)),
            pl.BlockSpec((d, tn), lambda j, i: (0, j)),
        ],
        out_specs=pl.BlockSpec((tm, tn), lambda j, i: (i, j)),
        compiler_params=_params("arbitrary", "arbitrary"),
        name="inproj",
    )(u, w_bf)


def _gates_body(u_ref, wg_ref, wgt_ref, brow_ref, bcol_ref, g_ref, gt_ref):
    u = u_ref[...]
    g = jnp.dot(u, wg_ref[...], preferred_element_type=F32) + brow_ref[...]
    lane = lax.broadcasted_iota(jnp.int32, g.shape, 1)
    is_forget = ((lane >> 2) & 1) == 1
    g_ref[...] = jnp.where(is_forget, -_softplus(-g), g)
    gt = lax.dot_general(wgt_ref[...], u, (((1,), (1,)), ((), ())), preferred_element_type=F32) + bcol_ref[...]
    row = lax.broadcasted_iota(jnp.int32, gt.shape, 0)
    is_forget_t = ((row >> 2) & 1) == 1
    gt_ref[...] = jnp.where(is_forget_t, -_softplus(-gt), gt)


def _gates(u, wg_pad, wgt, brow, bcol, tm=1024):
    n, d = u.shape
    return pl.pallas_call(
        _gates_body,
        out_shape=(jax.ShapeDtypeStruct((n, LANES), F32), jax.ShapeDtypeStruct((N_GATES, n), F32)),
        grid=(n // tm,),
        in_specs=[
            pl.BlockSpec((tm, d), lambda i: (i, 0)),
            pl.BlockSpec((d, LANES), lambda i: (0, 0)),
            pl.BlockSpec((N_GATES, d), lambda i: (0, 0)),
            pl.BlockSpec((1, LANES), lambda i: (0, 0)),
            pl.BlockSpec((N_GATES, 1), lambda i: (0, 0)),
        ],
        out_specs=(pl.BlockSpec((tm, LANES), lambda i: (i, 0)), pl.BlockSpec((N_GATES, tm), lambda i: (0, i))),
        compiler_params=_params("arbitrary"),
        name="gates",
    )(u, wg_pad, wgt, brow, bcol)


def _rglru_body(rx_ref, rgate_ref, cw_ref, cb_ref, wg_ref, bg_ref, lam_ref, h0_ref, y_ref, ht_ref,
                af_scr, uf_scr, ab_scr, ub_scr, hf_scr, *, seq_len, period):
    tc = rx_ref.shape[1]
    n_chunks = seq_len // CHUNK
    n_tiles = seq_len // SUBLANES
    sp = _softplus(-lam_ref[...])

    def gate_chunk(c, carry):
        rows = pl.ds(pl.multiple_of(c * CHUNK, CHUNK), CHUNK)
        x = rx_ref[rows, :]
        t = lax.broadcasted_iota(jnp.int32, (CHUNK, 1), 0) & (period - 1)
        xm2 = jnp.where(t >= 2, pltpu.roll(x, 2, 0), 0.0)
        xm1 = jnp.where(t >= 1, pltpu.roll(x, 1, 0), 0.0)
        xp1 = jnp.where(t <= period - 2, pltpu.roll(x, CHUNK - 1, 0), 0.0)
        xc = (cw_ref[0:1, :] * xm2 + cw_ref[1:2, :] * xm1 + cw_ref[2:3, :] * x + cw_ref[3:4, :] * xp1
              + cb_ref[...])
        g = jnp.dot(xc.astype(BF16), wg_ref[...], preferred_element_type=F32) + bg_ref[...]
        for d, (a_scr, u_scr) in enumerate(((af_scr, uf_scr), (ab_scr, ub_scr))):
            r = _sigmoid(g[:, (2 * d) * tc:(2 * d + 1) * tc])
            i = _sigmoid(g[:, (2 * d + 1) * tc:(2 * d + 2) * tc])
            log_a = (-RG_C) * r * sp[d:d + 1, :]
            a = jnp.exp(log_a)
            a_scr[rows, :] = a
            u_scr[rows, :] = jnp.sqrt(-jnp.tanh(log_a) * (a * a + 1.0)) * (i * xc)
        return carry

    lax.fori_loop(0, n_chunks, gate_chunk, 0)

    sub = lax.broadcasted_iota(jnp.int32, (SUBLANES, 1), 0)

    def tile_scan(a, u, reverse):
        for d in (1, 2, 4):
            if reverse:
                valid = sub < SUBLANES - d
                shift = SUBLANES - d
            else:
                valid = sub >= d
                shift = d
            a_sh = pltpu.roll(a, shift, 0)
            u_sh = pltpu.roll(u, shift, 0)
            u = jnp.where(valid, a * u_sh + u, u)
            a = jnp.where(valid, a * a_sh, a)
        return a, u

    def fwd(i, h_prev):
        rows = pl.ds(pl.multiple_of(i * SUBLANES, SUBLANES), SUBLANES)
        a, u = tile_scan(af_scr[rows, :], uf_scr[rows, :], False)
        h = a * h_prev + u
        hf_scr[rows, :] = h
        return jnp.broadcast_to(h[SUBLANES - 1:SUBLANES, :], h.shape)

    h0f = jnp.broadcast_to(h0_ref[0:1, :], (SUBLANES, tc))
    h_last = lax.fori_loop(0, n_tiles, fwd, h0f)
    ht_ref[0:1, :] = h_last[0:1, :]

    def bwd(k, h_next):
        i = n_tiles - 1 - k
        rows = pl.ds(pl.multiple_of(i * SUBLANES, SUBLANES), SUBLANES)
        a, u = tile_scan(ab_scr[rows, :], ub_scr[rows, :], True)
        h = a * h_next + u
        y = (hf_scr[rows, :] + h) * _gelu_tanh(rgate_ref[rows, :])
        y_ref[rows, :] = y.astype(y_ref.dtype)
        return jnp.broadcast_to(h[0:1, :], h.shape)

    h0b = jnp.broadcast_to(h0_ref[1:2, :], (SUBLANES, tc))
    h_first = lax.fori_loop(0, n_tiles, bwd, h0b)
    ht_ref[1:2, :] = h_first[0:1, :]


def _rglru(proj, row_block0, n_seq, seq_len, period, conv_w, conv_b, wg, bg, lam, h0, tc=256):
    n_ct = D_RG // tc
    body = functools.partial(_rglru_body, seq_len=seq_len, period=period)
    h0_map = (lambda s, g: (s, 0, g)) if h0.shape[0] > 1 else (lambda s, g: (0, 0, g))
    return pl.pallas_call(
        body,
        out_shape=(jax.ShapeDtypeStruct((n_seq * seq_len, D_RG), BF16),
                   jax.ShapeDtypeStruct((n_seq, 2, D_RG), F32)),
        grid=(n_seq, n_ct),
        in_specs=[
            pl.BlockSpec((seq_len, tc), lambda s, g: (row_block0 + s, g)),
            pl.BlockSpec((seq_len, tc), lambda s, g: (row_block0 + s, n_ct + g)),
            pl.BlockSpec((RG_CONV_W, tc), lambda s, g: (0, g)),
            pl.BlockSpec((1, tc), lambda s, g: (0, g)),
            pl.BlockSpec((None, tc, 4 * tc), lambda s, g: (g, 0, 0)),
            pl.BlockSpec((None, 1, 4 * tc), lambda s, g: (g, 0, 0)),
            pl.BlockSpec((2, tc), lambda s, g: (0, g)),
            pl.BlockSpec((None, 2, tc), h0_map),
        ],
        out_specs=(pl.BlockSpec((seq_len, tc), lambda s, g: (s, g)),
                   pl.BlockSpec((None, 2, tc), lambda s, g: (s, 0, g))),
        scratch_shapes=[pltpu.VMEM((seq_len, tc), F32) for _ in range(5)],
        compiler_params=_params("arbitrary", "arbitrary"),
        name="rglru",
    )(proj, proj, conv_w, conv_b, wg, bg, lam, h0)


def _mlstm_body(q_ref, k_ref, v_ref, og_ref, gc_ref, gr_ref, c0_ref, n0_ref, m0_ref, nw_ref,
                y_ref, *rest, seq_len, with_state):
    if with_state:
        c_out, n_out, m_out, hf_scr, c_scr, n_scr, m_scr = rest
    else:
        hf_scr, c_scr, n_scr, m_scr = rest
    L = CHUNK
    n_chunks = seq_len // L
    ii = lax.broadcasted_iota(jnp.int32, (L, L), 0)
    jj = lax.broadcasted_iota(jnp.int32, (L, L), 1)
    scale = ML_HD ** -0.5

    def chunk_step(rows, direction, update_state):
        mask = (jj <= ii) if direction == 0 else (jj >= ii)
        mask_t = (ii <= jj) if direction == 0 else (ii >= jj)
        ig_r = gr_ref[2 * direction:2 * direction + 1, rows]
        lf_r = gr_ref[2 * direction + 1:2 * direction + 2, rows]
        ig_c = gc_ref[rows, 2 * direction:2 * direction + 1]
        lf_c = gc_ref[rows, 2 * direction + 1:2 * direction + 2]
        b_col = jnp.sum(jnp.where(mask, lf_r, 0.0), axis=1, keepdims=True)
        b_row = jnp.sum(jnp.where(mask_t, lf_c, 0.0), axis=0, keepdims=True)
        total = jnp.sum(lf_r, axis=1, keepdims=True)
        m_prev = m_scr[0:1, 0:1]
        dmat = jnp.where(mask, b_col - b_row + ig_r, -jnp.inf)
        inter = b_col + m_prev
        m_t = jnp.maximum(inter, jnp.max(dmat, axis=1, keepdims=True))
        w_inter = jnp.exp(inter - m_t)
        q = q_ref[rows, :]
        k = k_ref[rows, :] * scale
        v = v_ref[rows, :]
        qb, kb, vb = q.astype(BF16), k.astype(BF16), v.astype(BF16)
        qk = lax.dot_general(qb, kb, (((1,), (1,)), ((), ())), preferred_element_type=F32)
        s = qk * jnp.exp(dmat - m_t)
        c_prev = c_scr[...]
        n_prev = n_scr[...]
        num = (w_inter * jnp.dot(qb, c_prev.astype(BF16), preferred_element_type=F32)
               + jnp.dot(s.astype(BF16), vb, preferred_element_type=F32))
        den = w_inter * jnp.sum(q * n_prev, axis=1, keepdims=True) + jnp.sum(s, axis=1, keepdims=True)
        h = num / jnp.maximum(jnp.abs(den), jnp.exp(-m_t))
        if update_state:
            g_row = total - b_row + ig_r
            g_col = total - b_col + ig_c
            m_new = jnp.maximum(total + m_prev, jnp.max(g_row, axis=1, keepdims=True))
            decay = jnp.exp(total + m_prev - m_new)
            kw = k * jnp.exp(g_col - m_new)
            c_scr[...] = decay * c_prev + lax.dot_general(
                kw.astype(BF16), vb, (((0,), (0,)), ((), ())), preferred_element_type=F32)
            n_scr[...] = decay * n_prev + jnp.sum(kw, axis=0, keepdims=True)
            m_scr[...] = jnp.broadcast_to(m_new, m_scr.shape)
        return h

    for direction in (0, 1):
        c_scr[...] = c0_ref[direction]
        n_scr[...] = n0_ref[direction]
        m_scr[...] = m0_ref[direction]

        def body(step, carry, direction=direction):
            c = step if direction == 0 else n_chunks - 1 - step
            rows = pl.ds(pl.multiple_of(c * L, L), L)
            h = chunk_step(rows, direction, with_state or n_chunks > 1)
            if direction == 0:
                hf_scr[rows, :] = h
            else:
                hm = hf_scr[rows, :] + h
                mu = jnp.mean(hm, axis=1, keepdims=True)
                hc = hm - mu
                var = jnp.mean(hc * hc, axis=1, keepdims=True)
                hn = hc * lax.rsqrt(var + LN_EPS) * nw_ref[...]
                y_ref[rows, :] = (_sigmoid(og_ref[rows, :]) * hn).astype(y_ref.dtype)
            return carry

        lax.fori_loop(0, n_chunks, body, 0)
        if with_state:
            c_out[direction] = c_scr[...]
            n_out[direction] = n_scr[...]
            m_out[direction] = m_scr[...]


def _mlstm(proj, row_block0, n_seq, seq_len, g_cols, g_rows, c0, n0, m0, norm_w, with_state):
    hd = ML_HD
    body = functools.partial(_mlstm_body, seq_len=seq_len, with_state=with_state)
    bcast = c0.shape[0] == 1
    smap = (lambda s: 0) if bcast else (lambda s: s)
    col = lambda base: (lambda s, h: (row_block0 + s, base // hd + h))
    out_shape = [jax.ShapeDtypeStruct((n_seq * seq_len, D_ML), BF16)]
    out_specs = [pl.BlockSpec((seq_len, hd), lambda s, h: (s, h))]
    if with_state:
        out_shape += [jax.ShapeDtypeStruct((n_seq, 2, ML_HEADS, hd, hd), F32),
                      jax.ShapeDtypeStruct((n_seq, 2, ML_HEADS, 1, hd), F32),
                      jax.ShapeDtypeStruct((n_seq, 2, ML_HEADS, 1, LANES), F32)]
        out_specs += [pl.BlockSpec((None, 2, None, hd, hd), lambda s, h: (s, 0, h, 0, 0)),
                      pl.BlockSpec((None, 2, None, 1, hd), lambda s, h: (s, 0, h, 0, 0)),
                      pl.BlockSpec((None, 2, None, 1, LANES), lambda s, h: (s, 0, h, 0, 0))]
    return pl.pallas_call(
        body,
        out_shape=tuple(out_shape),
        grid=(n_seq, ML_HEADS),
        in_specs=[
            pl.BlockSpec((seq_len, hd), col(COL_Q)),
            pl.BlockSpec((seq_len, hd), col(COL_K)),
            pl.BlockSpec((seq_len, hd), col(COL_V)),
            pl.BlockSpec((seq_len, hd), col(COL_O)),
            pl.BlockSpec((None, seq_len, 4), lambda s, h: (h, row_block0 + s, 0)),
            pl.BlockSpec((None, 4, seq_len), lambda s, h: (h, 0, row_block0 + s)),
            pl.BlockSpec((None, 2, None, hd, hd), lambda s, h: (smap(s), 0, h, 0, 0)),
            pl.BlockSpec((None, 2, None, 1, hd), lambda s, h: (smap(s), 0, h, 0, 0)),
            pl.BlockSpec((None, 2, None, 1, LANES), lambda s, h: (smap(s), 0, h, 0, 0)),
            pl.BlockSpec((1, hd), lambda s, h: (0, h)),
        ],
        out_specs=tuple(out_specs),
        scratch_shapes=[pltpu.VMEM((seq_len, hd), F32), pltpu.VMEM((hd, hd), F32),
                        pltpu.VMEM((1, hd), F32), pltpu.VMEM((1, LANES), F32)],
        compiler_params=_params("arbitrary", "arbitrary"),
        name="mlstm",
    )(proj, proj, proj, proj, g_cols, g_rows, c0, n0, m0, norm_w)


def _outproj_body(rg_ref, ml_ref, w_ref, x_ref, mod_ref, lng_ref, lnb_ref, rw_ref, rb_ref,
                  x1_ref, u2_ref, ti_ref, tg_ref, *, alpha, n_experts):
    mix = (jnp.dot(rg_ref[...], w_ref[0:D_RG, :], preferred_element_type=F32)
           + jnp.dot(ml_ref[...], w_ref[D_RG:D_MODEL, :], preferred_element_type=F32))
    z = alpha * x_ref[...] + mod_ref[2:3, :] * mix
    x1 = _layer_norm(z, lng_ref[...], lnb_ref[...])
    x1_ref[...] = x1
    u2 = x1 * (1.0 + mod_ref[4:5, :]) + mod_ref[3:4, :]
    u2_ref[...] = u2
    logits = jnp.dot(u2, rw_ref[...], preferred_element_type=F32, precision=HIGHEST) + rb_ref[...]
    lane = lax.broadcasted_iota(jnp.int32, logits.shape, 1)
    logits = jnp.where(lane < n_experts, logits, -jnp.inf)
    vals, idxs = [], []
    for _ in range(TOP_K):
        m = jnp.max(logits, axis=1, keepdims=True)
        idx = jnp.min(jnp.where(logits == m, lane, LANES), axis=1, keepdims=True)
        vals.append(m)
        idxs.append(idx)
        logits = jnp.where(lane == idx, -jnp.inf, logits)
    exps = [jnp.exp(v - vals[0]) for v in vals]
    inv = 1.0 / (exps[0] + exps[1] + exps[2] + exps[3])
    ti = jnp.zeros(lane.shape, jnp.int32)
    tg = jnp.zeros(lane.shape, F32)
    for k in range(TOP_K):
        ti = jnp.where(lane == k, idxs[k], ti)
        tg = jnp.where(lane == k, exps[k] * inv, tg)
    ti_ref[...] = ti
    tg_ref[...] = tg


def _outproj(rg_y, ml_y, w_out_bf, x, mod_l, ln_g, ln_b, rw_pad, rb_pad, alpha, n_experts, n_p, ds, tm=512):
    n, d = x.shape
    body = functools.partial(_outproj_body, alpha=alpha, n_experts=n_experts)
    row = lambda i: (i, 0)
    const = lambda i: (0, 0)
    return pl.pallas_call(
        body,
        out_shape=(jax.ShapeDtypeStruct((n, d), F32), jax.ShapeDtypeStruct((n, d), F32),
                   jax.ShapeDtypeStruct((n, LANES), jnp.int32), jax.ShapeDtypeStruct((n, LANES), F32)),
        grid=(n // tm,),
        in_specs=[
            pl.BlockSpec((tm, D_RG), row),
            pl.BlockSpec((tm, D_ML), row),
            pl.BlockSpec((d, d), const),
            pl.BlockSpec((tm, d), row),
            pl.BlockSpec((None, N_MOD, d), lambda i: (_cond_index(i * tm, n_p, ds), 0, 0)),
            pl.BlockSpec((1, d), const),
            pl.BlockSpec((1, d), const),
            pl.BlockSpec((d, LANES), const),
            pl.BlockSpec((1, LANES), const),
        ],
        out_specs=(pl.BlockSpec((tm, d), row), pl.BlockSpec((tm, d), row),
                   pl.BlockSpec((tm, LANES), row), pl.BlockSpec((tm, LANES), row)),
        compiler_params=_params("arbitrary"),
        name="outproj",
    )(rg_y, ml_y, w_out_bf, x, mod_l, ln_g, ln_b, rw_pad, rb_pad)


def _moe_body(order_ref, item_e_ref, item_start_ref, item_n_ref,
              u2_hbm, w1g_ref, w1l_ref, w2_ref, b1g_ref, b1l_ref, b2_ref,
              y4_hbm,
              xf_scr, xb_scr, acc_scr, w1g_scr, w1l_scr, w2_scr, gsem, ssem, *, n_assign, n_ft):
    s = pl.program_id(0)
    j = pl.program_id(1)
    n_rows = item_n_ref[s]
    start = item_start_ref[s]
    n_blocks = (n_rows + MOE_RB - 1) // MOE_RB
    d = acc_scr.shape[1]

    def gather_copy(r):
        o = order_ref[jnp.minimum(start + r, n_assign - 1)]
        tok = lax.shift_right_logical(o, 2)
        return pltpu.make_async_copy(u2_hbm.at[pl.ds(tok, 1)], xf_scr.at[pl.ds(r, 1)], gsem)

    def scatter_copy(r):
        o = order_ref[start + r]
        return pltpu.make_async_copy(acc_scr.at[pl.ds(r, 1)], y4_hbm.at[pl.ds(o, 1)], ssem)

    def start_all(make, count):
        def f(r, c):
            make(r).start()
            return c
        lax.fori_loop(0, count, f, 0)

    def wait_all(make, count):
        def f(r, c):
            make(r).wait()
            return c
        lax.fori_loop(0, count, f, 0)

    @pl.when(n_rows > 0)
    def _item():
        @pl.when(j == 0)
        def _load_rows():
            start_all(gather_copy, n_blocks * MOE_RB)
            wait_all(gather_copy, n_blocks * MOE_RB)

            def prep(b, c):
                rows = pl.ds(pl.multiple_of(b * MOE_RB, MOE_RB), MOE_RB)
                xb_scr[rows, :] = xf_scr[rows, :].astype(BF16)
                acc_scr[rows, :] = jnp.broadcast_to(b2_ref[...], (MOE_RB, d))
                return c
            lax.fori_loop(0, n_blocks, prep, 0)

        w1g_scr[...] = w1g_ref[...].astype(BF16)
        w1l_scr[...] = w1l_ref[...].astype(BF16)
        w2_scr[...] = w2_ref[...].astype(BF16)

        def block(b, c):
            rows = pl.ds(pl.multiple_of(b * MOE_RB, MOE_RB), MOE_RB)
            x = xb_scr[rows, :]
            hg = jnp.dot(x, w1g_scr[...], preferred_element_type=F32) + b1g_ref[...]
            hl = jnp.dot(x, w1l_scr[...], preferred_element_type=F32) + b1l_ref[...]
            hg = jnp.minimum(hg, SWIGLU_LIMIT)
            hl = jnp.clip(hl, -SWIGLU_LIMIT, SWIGLU_LIMIT)
            act = hg * _sigmoid(SWIGLU_ALPHA * hg) * (hl + 1.0)
            acc_scr[rows, :] += jnp.dot(act.astype(BF16), w2_scr[...], preferred_element_type=F32)
            return c
        lax.fori_loop(0, n_blocks, block, 0)

        @pl.when(j == n_ft - 1)
        def _store_rows():
            start_all(scatter_copy, n_rows)
            wait_all(scatter_copy, n_rows)


def _moe_items(top_i, n_experts, max_items):
    n_assign = top_i.size
    e_flat = top_i.reshape(n_assign)
    order = jnp.argsort(e_flat, stable=True).astype(jnp.int32)
    counts = jnp.sum((e_flat[:, None] == jnp.arange(n_experts, dtype=jnp.int32)[None, :]).astype(jnp.int32), axis=0)
    starts = jnp.cumsum(counts) - counts
    items_per_e = (counts + MOE_CAP - 1) // MOE_CAP
    item_ends = jnp.cumsum(items_per_e)
    total = item_ends[-1]
    slot = jnp.arange(max_items, dtype=jnp.int32)
    valid = slot < total
    slot_c = jnp.minimum(slot, total - 1)
    e_s = jnp.minimum(jnp.searchsorted(item_ends, slot_c, side="right"), n_experts - 1).astype(jnp.int32)
    local = slot_c - (item_ends[e_s] - items_per_e[e_s])
    start = starts[e_s] + local * MOE_CAP
    n_rows = jnp.clip(counts[e_s] - local * MOE_CAP, 0, MOE_CAP)
    return (order, e_s, jnp.where(valid, start, 0).astype(jnp.int32),
            jnp.where(valid, n_rows, 0).astype(jnp.int32))


def _moe(u2, top_i, w1, b1, w2, b2):
    n, d = u2.shape
    n_experts, _, two_de = w1.shape
    de = two_de // 2
    n_ft = de // MOE_TF
    n_assign = n * TOP_K
    max_items = n_experts + -(-n_assign // MOE_CAP)
    order, item_e, item_start, item_n = _moe_items(top_i, n_experts, max_items)
    body = functools.partial(_moe_body, n_assign=n_assign, n_ft=n_ft)

    def jt(s, j, item_n_ref):
        return jnp.where(item_n_ref[s] > 0, j, n_ft - 1)

    grid_spec = pltpu.PrefetchScalarGridSpec(
        num_scalar_prefetch=4,
        grid=(max_items, n_ft),
        in_specs=[
            pl.BlockSpec(memory_space=pl.ANY),
            pl.BlockSpec((None, d, MOE_TF), lambda s, j, o, ie, ist, inn: (ie[s], 0, jt(s, j, inn))),
            pl.BlockSpec((None, d, MOE_TF), lambda s, j, o, ie, ist, inn: (ie[s], 0, n_ft + jt(s, j, inn))),
            pl.BlockSpec((None, MOE_TF, d), lambda s, j, o, ie, ist, inn: (ie[s], jt(s, j, inn), 0)),
            pl.BlockSpec((None, 1, MOE_TF), lambda s, j, o, ie, ist, inn: (ie[s], 0, jt(s, j, inn))),
            pl.BlockSpec((None, 1, MOE_TF), lambda s, j, o, ie, ist, inn: (ie[s], 0, n_ft + jt(s, j, inn))),
            pl.BlockSpec((None, 1, d), lambda s, j, o, ie, ist, inn: (ie[s], 0, 0)),
        ],
        out_specs=pl.BlockSpec(memory_space=pl.ANY),
        scratch_shapes=[
            pltpu.VMEM((MOE_CAP, d), F32), pltpu.VMEM((MOE_CAP, d), BF16), pltpu.VMEM((MOE_CAP, d), F32),
            pltpu.VMEM((d, MOE_TF), BF16), pltpu.VMEM((d, MOE_TF), BF16), pltpu.VMEM((MOE_TF, d), BF16),
            pltpu.SemaphoreType.DMA, pltpu.SemaphoreType.DMA,
        ],
    )
    return pl.pallas_call(
        body,
        out_shape=jax.ShapeDtypeStruct((n_assign, d), F32),
        grid_spec=grid_spec,
        compiler_params=_params("arbitrary", "arbitrary"),
        name="moe",
    )(order, item_e, item_start, item_n, u2, w1, w1, w2,
      b1.reshape(n_experts, 1, two_de), b1.reshape(n_experts, 1, two_de), b2.reshape(n_experts, 1, d))


def _combine_body(x1_ref, y4_ref, tg_ref, mod_ref, modn_ref, lng_ref, lnb_ref, x2_ref, un_ref, *, alpha):
    d = x1_ref.shape[1]
    tg = tg_ref[...]
    ff = tg[:, 0:1] * y4_ref[:, 0:d]
    for k in range(1, TOP_K):
        ff = ff + tg[:, k:k + 1] * y4_ref[:, k * d:(k + 1) * d]
    z = alpha * x1_ref[...] + mod_ref[5:6, :] * ff
    x2 = _layer_norm(z, lng_ref[...], lnb_ref[...])
    x2_ref[...] = x2
    un_ref[...] = (x2 * (1.0 + modn_ref[1:2, :]) + modn_ref[0:1, :]).astype(BF16)


def _combine(x1, y4, tg, mod_l, mod_next, ln_g, ln_b, alpha, n_p, ds, tm=256):
    n, d = x1.shape
    body = functools.partial(_combine_body, alpha=alpha)
    row = lambda i: (i, 0)
    const = lambda i: (0, 0)
    cond = lambda i: (_cond_index(i * tm, n_p, ds), 0, 0)
    return pl.pallas_call(
        body,
        out_shape=(jax.ShapeDtypeStruct((n, d), F32), jax.ShapeDtypeStruct((n, d), BF16)),
        grid=(n // tm,),
        in_specs=[
            pl.BlockSpec((tm, d), row),
            pl.BlockSpec((tm, TOP_K * d), row),
            pl.BlockSpec((tm, LANES), row),
            pl.BlockSpec((None, N_MOD, d), cond),
            pl.BlockSpec((None, N_MOD, d), cond),
            pl.BlockSpec((1, d), const),
            pl.BlockSpec((1, d), const),
        ],
        out_specs=(pl.BlockSpec((tm, d), row), pl.BlockSpec((tm, d), row)),
        compiler_params=_params("arbitrary"),
        name="combine",
    )(x1, y4.reshape(n, TOP_K * d), tg, mod_l, mod_next, ln_g, ln_b)


def _block_diag_tiles(w, tc):
    depth, nb, bw, _ = w.shape
    per = tc // bw
    wt = w.reshape(depth, nb // per, per, bw, bw)
    eye = jnp.eye(per, dtype=w.dtype)
    full = jnp.einsum("lgajk,ab->lgajbk", wt, eye)
    return full.reshape(depth, nb // per, tc, tc)


def _rg_gate_weights(rg_wa, rg_ba, rg_wx, rg_bx, tc=256):
    depth = rg_wa.shape[0]
    n_ct = D_RG // tc
    tiles = [_block_diag_tiles(w[:, d], tc) for d in (0, 1) for w in (rg_wa, rg_wx)]
    wg = jnp.concatenate(tiles, axis=-1).astype(BF16)
    biases = [b[:, d].reshape(depth, n_ct, 1, tc) for d in (0, 1) for b in (rg_ba, rg_bx)]
    bg = jnp.concatenate(biases, axis=-1)
    return wg, bg


def kernel(x_prompt, x_sample, state_rglru, state_mlstm_C, state_mlstm_n, state_mlstm_m, c, c_ctx, w_ada, b_ada, w_in, ml_gate_b, rg_conv_w, rg_conv_b, rg_wa, rg_ba, rg_wx, rg_bx, rg_lambda, ml_norm_w, w_out, ln1_g, ln1_b, router_w, router_b, moe_w1, moe_b1, moe_w2, moe_b2, ln2_g, ln2_b):
    depth = w_ada.shape[0]
    bsz, seq, d = x_prompt.shape
    dbs, dseq, _ = x_sample.shape
    n_p, n_s = bsz * seq, dbs * dseq
    n_experts = router_w.shape[-1]
    alpha = float((2 * depth) ** 0.25)
    assert d == D_MODEL and seq == CHUNK and dseq % CHUNK == 0 and n_p % dseq == 0

    x = jnp.concatenate([x_prompt.reshape(n_p, d), x_sample.reshape(n_s, d)], axis=0)
    n_cond = 1 + dbs
    conds = jnp.concatenate([c_ctx[None, :], c, jnp.zeros((SUBLANES - n_cond, d), F32)], axis=0)
    mod = _ada(conds, w_ada, b_ada)
    mod = mod[:, :n_cond].reshape(depth, n_cond, N_MOD, d)

    w_in_bf = w_in.astype(BF16)
    w_g = w_in[:, :, D_PROJ:]
    wg_pad = jnp.pad(w_g, ((0, 0), (0, 0), (0, LANES - N_GATES))).astype(BF16)
    wgt = jnp.swapaxes(w_g, 1, 2).astype(BF16)
    gb_row = jnp.pad(ml_gate_b, ((0, 0), (0, LANES - N_GATES))).reshape(depth, 1, LANES)
    gb_col = ml_gate_b.reshape(depth, N_GATES, 1)
    rg_wg, rg_bg = _rg_gate_weights(rg_wa, rg_ba, rg_wx, rg_bx)
    w_out_bf = w_out.astype(BF16)
    rw_pad = jnp.pad(router_w, ((0, 0), (0, 0), (0, LANES - n_experts)))
    rb_pad = jnp.pad(router_b, ((0, 0), (0, LANES - n_experts))).reshape(depth, 1, LANES)

    zero_rg = jnp.zeros((1, 2, D_RG), F32)
    zero_c = jnp.zeros((1, 2, ML_HEADS, ML_HD, ML_HD), F32)
    zero_n = jnp.zeros((1, 2, ML_HEADS, 1, ML_HD), F32)
    zero_m = jnp.zeros((1, 2, ML_HEADS, 1, LANES), F32)

    u = _modulate(x, mod[0], n_p, dseq)
    states = []
    for l in range(depth):
        proj = _inproj(u, w_in_bf[l])
        g, gt = _gates(u, wg_pad[l], wgt[l], gb_row[l], gb_col[l])
        g_cols = g[:, :N_GATES].reshape(n_p + n_s, 4, ML_HEADS).transpose(2, 0, 1)
        g_rows = gt.reshape(4, ML_HEADS, n_p + n_s).transpose(1, 0, 2)

        conv_b = rg_conv_b[l].reshape(1, D_RG)
        rg_p, ht_p = _rglru(proj, 0, bsz, seq, seq, rg_conv_w[l], conv_b, rg_wg[l], rg_bg[l], rg_lambda[l], zero_rg)
        rg_s, _ = _rglru(proj, n_p // dseq, dbs, dseq, GRID_W, rg_conv_w[l], conv_b, rg_wg[l], rg_bg[l],
                         rg_lambda[l], state_rglru[:, l])

        norm_w = ml_norm_w[l].reshape(1, D_ML)
        ml_p, c_p, nn_p, m_p = _mlstm(proj, 0, bsz, seq, g_cols, g_rows, zero_c, zero_n, zero_m, norm_w, True)
        m0_s = jnp.broadcast_to(state_mlstm_m[:, l][..., None, None], (dbs, 2, ML_HEADS, 1, LANES))
        (ml_s,) = _mlstm(proj, n_p // dseq, dbs, dseq, g_cols, g_rows, state_mlstm_C[:, l],
                         state_mlstm_n[:, l][:, :, :, None, :], m0_s, norm_w, False)
        states.append((ht_p, c_p, nn_p[:, :, :, 0, :], m_p[:, :, :, 0, 0]))

        rg_y = jnp.concatenate([rg_p, rg_s], axis=0)
        ml_y = jnp.concatenate([ml_p, ml_s], axis=0)
        x1, u2, ti, tg = _outproj(rg_y, ml_y, w_out_bf[l], x, mod[l], ln1_g[l].reshape(1, d), ln1_b[l].reshape(1, d),
                                  rw_pad[l], rb_pad[l], alpha, n_experts, n_p, dseq)
        y4 = _moe(u2, ti[:, :TOP_K], moe_w1[l], moe_b1[l], moe_w2[l], moe_b2[l])
        x, u = _combine(x1, y4, tg, mod[l], mod[min(l + 1, depth - 1)], ln2_g[l].reshape(1, d),
                        ln2_b[l].reshape(1, d), alpha, n_p, dseq)

    new_rglru = jnp.stack([s[0] for s in states], axis=1)
    new_c = jnp.stack([s[1] for s in states], axis=1)
    new_n = jnp.stack([s[2] for s in states], axis=1)
    new_m = jnp.stack([s[3] for s in states], axis=1)
    return (x[:n_p].reshape(bsz, seq, d), x[n_p:].reshape(dbs, dseq, d), new_rglru, new_c, new_n, new_m)
```

```python
import functools

import jax
import jax.numpy as jnp
from jax import lax
from jax.experimental import pallas as pl
from jax.experimental.pallas import tpu as pltpu

F32 = jnp.float32
BF16 = jnp.bfloat16

D_MODEL = 2048
D_RG = 1024
RG_BLOCK_W = 64
RG_CONV_W = 4
RG_C = 8.0
GRID_W = 64
D_ML = 1024
ML_HEADS = 4
ML_HD = 256
TOP_K = 4
SWIGLU_LIMIT = 7.0
SWIGLU_ALPHA = 1.702
LN_EPS = 1e-5
N_MOD = 6
CHUNK = 256
SUBLANES = 8
LANES = 128
N_GATES = 4 * ML_HEADS
COL_Q = 2 * D_RG
COL_K = COL_Q + D_ML
COL_V = COL_K + D_ML
COL_O = COL_V + D_ML
D_PROJ = COL_O + D_ML

TOKEN_TILE_ROWS = D_MODEL // 2 // LANES
MOE_CAP = 1280
MOE_RB = 256
MOE_TF = 256
MOE_DMA_UNROLL = 16
VMEM_LIMIT = 56 * 1024 * 1024
HIGHEST = lax.Precision.HIGHEST


def _pack_bf16_pair(lo, hi):
    lo_bits = lax.bitcast_convert_type(lo.astype(BF16).astype(F32), jnp.uint32)
    hi_bits = lax.bitcast_convert_type(hi.astype(BF16).astype(F32), jnp.uint32)
    return (lo_bits >> 16) | (hi_bits & jnp.uint32(0xFFFF0000))


def _unpack_bf16_pair(w):
    lo = lax.bitcast_convert_type(w << 16, F32)
    hi = lax.bitcast_convert_type(w & jnp.uint32(0xFFFF0000), F32)
    return lo, hi


def _sigmoid(x):
    return 1.0 / (1.0 + jnp.exp(-x))


def _softplus(x):
    return jnp.maximum(x, 0.0) + jnp.log1p(jnp.exp(-jnp.abs(x)))


def _gelu_tanh(x):
    return 0.5 * x * (1.0 + jnp.tanh(0.7978845608028654 * (x + 0.044715 * (x * x * x))))


def _layer_norm(z, g, b):
    mu = jnp.mean(z, axis=-1, keepdims=True)
    zc = z - mu
    var = jnp.mean(zc * zc, axis=-1, keepdims=True)
    return zc * lax.rsqrt(var + LN_EPS) * g + b


def _params(*sem):
    return pltpu.CompilerParams(dimension_semantics=sem, vmem_limit_bytes=VMEM_LIMIT)


def _cond_index(row_start, n_p, ds):
    return jnp.where(row_start < n_p, 0, 1 + (row_start - n_p) // ds)


def _ada_body(c_ref, w_ref, b_ref, o_ref):
    c = c_ref[...]
    s = c * _sigmoid(c)
    o_ref[...] = jnp.dot(s, w_ref[...], preferred_element_type=F32, precision=HIGHEST) + b_ref[...]


def _ada(conds, w_ada, b_ada, tn=1024):
    depth, d, n6 = w_ada.shape
    nc = conds.shape[0]
    return pl.pallas_call(
        _ada_body,
        out_shape=jax.ShapeDtypeStruct((depth, nc, n6), F32),
        grid=(depth, n6 // tn),
        in_specs=[
            pl.BlockSpec((nc, d), lambda l, j: (0, 0)),
            pl.BlockSpec((None, d, tn), lambda l, j: (l, 0, j)),
            pl.BlockSpec((None, 1, tn), lambda l, j: (l, 0, j)),
        ],
        out_specs=pl.BlockSpec((None, nc, tn), lambda l, j: (l, 0, j)),
        compiler_params=_params("arbitrary", "arbitrary"),
        name="ada",
    )(conds, w_ada, b_ada.reshape(depth, 1, n6))


def _modulate_body(x_ref, mod_ref, u_ref):
    u_ref[...] = (x_ref[...] * (1.0 + mod_ref[1:2, :]) + mod_ref[0:1, :]).astype(BF16)


def _modulate(x, mod_l, n_p, ds, tm=512):
    n, d = x.shape
    return pl.pallas_call(
        _modulate_body,
        out_shape=jax.ShapeDtypeStruct((n, d), BF16),
        grid=(n // tm,),
        in_specs=[
            pl.BlockSpec((tm, d), lambda i: (i, 0)),
            pl.BlockSpec((None, N_MOD, d), lambda i: (_cond_index(i * tm, n_p, ds), 0, 0)),
        ],
        out_specs=pl.BlockSpec((tm, d), lambda i: (i, 0)),
        compiler_params=_params("arbitrary"),
        name="modulate",
    )(x, mod_l)


def _inproj_body(u_ref, w_ref, o_ref):
    o_ref[...] = jnp.dot(u_ref[...], w_ref[...], preferred_element_type=F32)


def _inproj(u, w_bf, tm=1024, tn=768):
    n, d = u.shape
    return pl.pallas_call(
        _inproj_body,
        out_shape=jax.ShapeDtypeStruct((n, D_PROJ), F32),
        grid=(D_PROJ // tn, n // tm),
        in_specs=[
            pl.BlockSpec((tm, d), lambda j, i: (i, 0)),
            pl.BlockSpec((d, tn), lambda j, i: (0, j)),
        ],
        out_specs=pl.BlockSpec((tm, tn), lambda j, i: (i, j)),
        compiler_params=_params("arbitrary", "arbitrary"),
        name="inproj",
    )(u, w_bf)


def _gates_body(u_ref, wg_ref, wgt_ref, brow_ref, bcol_ref, g_ref, gt_ref):
    u = u_ref[...]
    g = jnp.dot(u, wg_ref[...], preferred_element_type=F32) + brow_ref[...]
    lane = lax.broadcasted_iota(jnp.int32, g.shape, 1)
    is_forget = ((lane >> 2) & 1) == 1
    g_ref[...] = jnp.where(is_forget, -_softplus(-g), g)
    gt = lax.dot_general(wgt_ref[...], u, (((1,), (1,)), ((), ())), preferred_element_type=F32) + bcol_ref[...]
    row = lax.broadcasted_iota(jnp.int32, gt.shape, 0)
    is_forget_t = ((row >> 2) & 1) == 1
    gt_ref[...] = jnp.where(is_forget_t, -_softplus(-gt), gt)


def _gates(u, wg_pad, wgt, brow, bcol, tm=1024):
    n, d = u.shape
    return pl.pallas_call(
        _gates_body,
        out_shape=(jax.ShapeDtypeStruct((n, LANES), F32), jax.ShapeDtypeStruct((N_GATES, n), F32)),
        grid=(n // tm,),
        in_specs=[
            pl.BlockSpec((tm, d), lambda i: (i, 0)),
            pl.BlockSpec((d, LANES), lambda i: (0, 0)),
            pl.BlockSpec((N_GATES, d), lambda i: (0, 0)),
            pl.BlockSpec((1, LANES), lambda i: (0, 0)),
            pl.BlockSpec((N_GATES, 1), lambda i: (0, 0)),
        ],
        out_specs=(pl.BlockSpec((tm, LANES), lambda i: (i, 0)), pl.BlockSpec((N_GATES, tm), lambda i: (0, i))),
        compiler_params=_params("arbitrary"),
        name="gates",
    )(u, wg_pad, wgt, brow, bcol)


def _rglru_body(rx_ref, rgate_ref, cw_ref, cb_ref, wg_ref, bg_ref, lam_ref, h0_ref, y_ref, ht_ref,
                af_scr, uf_scr, ab_scr, ub_scr, hf_scr, *, seq_len, period):
    tc = rx_ref.shape[1]
    n_chunks = seq_len // CHUNK
    n_tiles = seq_len // SUBLANES
    sp = _softplus(-lam_ref[...])

    def gate_chunk(c, carry):
        rows = pl.ds(pl.multiple_of(c * CHUNK, CHUNK), CHUNK)
        x = rx_ref[rows, :]
        t = lax.broadcasted_iota(jnp.int32, (CHUNK, 1), 0) & (period - 1)
        xm2 = jnp.where(t >= 2, pltpu.roll(x, 2, 0), 0.0)
        xm1 = jnp.where(t >= 1, pltpu.roll(x, 1, 0), 0.0)
        xp1 = jnp.where(t <= period - 2, pltpu.roll(x, CHUNK - 1, 0), 0.0)
        xc = (cw_ref[0:1, :] * xm2 + cw_ref[1:2, :] * xm1 + cw_ref[2:3, :] * x + cw_ref[3:4, :] * xp1
              + cb_ref[...])
        g = jnp.dot(xc.astype(BF16), wg_ref[...], preferred_element_type=F32) + bg_ref[...]
        for d, (a_scr, u_scr) in enumerate(((af_scr, uf_scr), (ab_scr, ub_scr))):
            r = _sigmoid(g[:, (2 * d) * tc:(2 * d + 1) * tc])
            i = _sigmoid(g[:, (2 * d + 1) * tc:(2 * d + 2) * tc])
            log_a = (-RG_C) * r * sp[d:d + 1, :]
            a = jnp.exp(log_a)
            a_scr[rows, :] = a
            u_scr[rows, :] = jnp.sqrt(-jnp.tanh(log_a) * (a * a + 1.0)) * (i * xc)
        return carry

    lax.fori_loop(0, n_chunks, gate_chunk, 0)

    sub = lax.broadcasted_iota(jnp.int32, (SUBLANES, 1), 0)

    def tile_scan(a, u, reverse):
        for d in (1, 2, 4):
            if reverse:
                valid = sub < SUBLANES - d
                shift = SUBLANES - d
            else:
                valid = sub >= d
                shift = d
            a_sh = pltpu.roll(a, shift, 0)
            u_sh = pltpu.roll(u, shift, 0)
            u = jnp.where(valid, a * u_sh + u, u)
            a = jnp.where(valid, a * a_sh, a)
        return a, u

    def fwd(i, h_prev):
        rows = pl.ds(pl.multiple_of(i * SUBLANES, SUBLANES), SUBLANES)
        a, u = tile_scan(af_scr[rows, :], uf_scr[rows, :], False)
        h = a * h_prev + u
        hf_scr[rows, :] = h
        return jnp.broadcast_to(h[SUBLANES - 1:SUBLANES, :], h.shape)

    h0f = jnp.broadcast_to(h0_ref[0:1, :], (SUBLANES, tc))
    h_last = lax.fori_loop(0, n_tiles, fwd, h0f)
    ht_ref[0:1, :] = h_last[0:1, :]

    def bwd(k, h_next):
        i = n_tiles - 1 - k
        rows = pl.ds(pl.multiple_of(i * SUBLANES, SUBLANES), SUBLANES)
        a, u = tile_scan(ab_scr[rows, :], ub_scr[rows, :], True)
        h = a * h_next + u
        y = (hf_scr[rows, :] + h) * _gelu_tanh(rgate_ref[rows, :])
        y_ref[rows, :] = y.astype(y_ref.dtype)
        return jnp.broadcast_to(h[0:1, :], h.shape)

    h0b = jnp.broadcast_to(h0_ref[1:2, :], (SUBLANES, tc))
    h_first = lax.fori_loop(0, n_tiles, bwd, h0b)
    ht_ref[1:2, :] = h_first[0:1, :]


def _rglru(proj, row_block0, n_seq, seq_len, period, conv_w, conv_b, wg, bg, lam, h0, tc=256):
    n_ct = D_RG // tc
    body = functools.partial(_rglru_body, seq_len=seq_len, period=period)
    h0_map = (lambda s, g: (s, 0, g)) if h0.shape[0] > 1 else (lambda s, g: (0, 0, g))
    return pl.pallas_call(
        body,
        out_shape=(jax.ShapeDtypeStruct((n_seq * seq_len, D_RG), BF16),
                   jax.ShapeDtypeStruct((n_seq, 2, D_RG), F32)),
        grid=(n_seq, n_ct),
        in_specs=[
            pl.BlockSpec((seq_len, tc), lambda s, g: (row_block0 + s, g)),
            pl.BlockSpec((seq_len, tc), lambda s, g: (row_block0 + s, n_ct + g)),
            pl.BlockSpec((RG_CONV_W, tc), lambda s, g: (0, g)),
            pl.BlockSpec((1, tc), lambda s, g: (0, g)),
            pl.BlockSpec((None, tc, 4 * tc), lambda s, g: (g, 0, 0)),
            pl.BlockSpec((None, 1, 4 * tc), lambda s, g: (g, 0, 0)),
            pl.BlockSpec((2, tc), lambda s, g: (0, g)),
            pl.BlockSpec((None, 2, tc), h0_map),
        ],
        out_specs=(pl.BlockSpec((seq_len, tc), lambda s, g: (s, g)),
                   pl.BlockSpec((None, 2, tc), lambda s, g: (s, 0, g))),
        scratch_shapes=[pltpu.VMEM((seq_len, tc), F32) for _ in range(5)],
        compiler_params=_params("arbitrary", "arbitrary"),
        name="rglru",
    )(proj, proj, conv_w, conv_b, wg, bg, lam, h0)


def _mlstm_body(q_ref, k_ref, v_ref, og_ref, gc_ref, gr_ref, c0_ref, n0_ref, m0_ref, nw_ref,
                y_ref, *rest, seq_len, with_state):
    if with_state:
        c_out, n_out, m_out, hf_scr, c_scr, n_scr, m_scr = rest
    else:
        hf_scr, c_scr, n_scr, m_scr = rest
    L = CHUNK
    n_chunks = seq_len // L
    ii = lax.broadcasted_iota(jnp.int32, (L, L), 0)
    jj = lax.broadcasted_iota(jnp.int32, (L, L), 1)
    scale = ML_HD ** -0.5

    def chunk_step(rows, direction, update_state):
        mask = (jj <= ii) if direction == 0 else (jj >= ii)
        mask_t = (ii <= jj) if direction == 0 else (ii >= jj)
        ig_r = gr_ref[2 * direction:2 * direction + 1, rows]
        lf_r = gr_ref[2 * direction + 1:2 * direction + 2, rows]
        ig_c = gc_ref[rows, 2 * direction:2 * direction + 1]
        lf_c = gc_ref[rows, 2 * direction + 1:2 * direction + 2]
        b_col = jnp.sum(jnp.where(mask, lf_r, 0.0), axis=1, keepdims=True)
        b_row = jnp.sum(jnp.where(mask_t, lf_c, 0.0), axis=0, keepdims=True)
        total = jnp.sum(lf_r, axis=1, keepdims=True)
        m_prev = m_scr[0:1, 0:1]
        dmat = jnp.where(mask, b_col - b_row + ig_r, -jnp.inf)
        inter = b_col + m_prev
        m_t = jnp.maximum(inter, jnp.max(dmat, axis=1, keepdims=True))
        w_inter = jnp.exp(inter - m_t)
        q = q_ref[rows, :]
        k = k_ref[rows, :] * scale
        v = v_ref[rows, :]
        qb, kb, vb = q.astype(BF16), k.astype(BF16), v.astype(BF16)
        qk = lax.dot_general(qb, kb, (((1,), (1,)), ((), ())), preferred_element_type=F32)
        s = qk * jnp.exp(dmat - m_t)
        c_prev = c_scr[...]
        n_prev = n_scr[...]
        num = (w_inter * jnp.dot(qb, c_prev.astype(BF16), preferred_element_type=F32)
               + jnp.dot(s.astype(BF16), vb, preferred_element_type=F32))
        den = w_inter * jnp.sum(q * n_prev, axis=1, keepdims=True) + jnp.sum(s, axis=1, keepdims=True)
        h = num / jnp.maximum(jnp.abs(den), jnp.exp(-m_t))
        if update_state:
            g_row = total - b_row + ig_r
            g_col = total - b_col + ig_c
            m_new = jnp.maximum(total + m_prev, jnp.max(g_row, axis=1, keepdims=True))
            decay = jnp.exp(total + m_prev - m_new)
            kw = k * jnp.exp(g_col - m_new)
            c_scr[...] = decay * c_prev + lax.dot_general(
                kw.astype(BF16), vb, (((0,), (0,)), ((), ())), preferred_element_type=F32)
            n_scr[...] = decay * n_prev + jnp.sum(kw, axis=0, keepdims=True)
            m_scr[...] = jnp.broadcast_to(m_new, m_scr.shape)
        return h

    for direction in (0, 1):
        c_scr[...] = c0_ref[direction]
        n_scr[...] = n0_ref[direction]
        m_scr[...] = m0_ref[direction]

        def body(step, carry, direction=direction):
            c = step if direction == 0 else n_chunks - 1 - step
            rows = pl.ds(pl.multiple_of(c * L, L), L)
            h = chunk_step(rows, direction, with_state or n_chunks > 1)
            if direction == 0:
                hf_scr[rows, :] = h
            else:
                hm = hf_scr[rows, :] + h
                mu = jnp.mean(hm, axis=1, keepdims=True)
                hc = hm - mu
                var = jnp.mean(hc * hc, axis=1, keepdims=True)
                hn = hc * lax.rsqrt(var + LN_EPS) * nw_ref[...]
                y_ref[rows, :] = (_sigmoid(og_ref[rows, :]) * hn).astype(y_ref.dtype)
            return carry

        lax.fori_loop(0, n_chunks, body, 0)
        if with_state:
            c_out[direction] = c_scr[...]
            n_out[direction] = n_scr[...]
            m_out[direction] = m_scr[...]


def _mlstm(proj, row_block0, n_seq, seq_len, g_cols, g_rows, c0, n0, m0, norm_w, with_state):
    hd = ML_HD
    body = functools.partial(_mlstm_body, seq_len=seq_len, with_state=with_state)
    bcast = c0.shape[0] == 1
    smap = (lambda s: 0) if bcast else (lambda s: s)
    col = lambda base: (lambda s, h: (row_block0 + s, base // hd + h))
    out_shape = [jax.ShapeDtypeStruct((n_seq * seq_len, D_ML), BF16)]
    out_specs = [pl.BlockSpec((seq_len, hd), lambda s, h: (s, h))]
    if with_state:
        out_shape += [jax.ShapeDtypeStruct((n_seq, 2, ML_HEADS, hd, hd), F32),
                      jax.ShapeDtypeStruct((n_seq, 2, ML_HEADS, 1, hd), F32),
                      jax.ShapeDtypeStruct((n_seq, 2, ML_HEADS, 1, LANES), F32)]
        out_specs += [pl.BlockSpec((None, 2, None, hd, hd), lambda s, h: (s, 0, h, 0, 0)),
                      pl.BlockSpec((None, 2, None, 1, hd), lambda s, h: (s, 0, h, 0, 0)),
                      pl.BlockSpec((None, 2, None, 1, LANES), lambda s, h: (s, 0, h, 0, 0))]
    return pl.pallas_call(
        body,
        out_shape=tuple(out_shape),
        grid=(n_seq, ML_HEADS),
        in_specs=[
            pl.BlockSpec((seq_len, hd), col(COL_Q)),
            pl.BlockSpec((seq_len, hd), col(COL_K)),
            pl.BlockSpec((seq_len, hd), col(COL_V)),
            pl.BlockSpec((seq_len, hd), col(COL_O)),
            pl.BlockSpec((None, seq_len, 4), lambda s, h: (h, row_block0 + s, 0)),
            pl.BlockSpec((None, 4, seq_len), lambda s, h: (h, 0, row_block0 + s)),
            pl.BlockSpec((None, 2, None, hd, hd), lambda s, h: (smap(s), 0, h, 0, 0)),
            pl.BlockSpec((None, 2, None, 1, hd), lambda s, h: (smap(s), 0, h, 0, 0)),
            pl.BlockSpec((None, 2, None, 1, LANES), lambda s, h: (smap(s), 0, h, 0, 0)),
            pl.BlockSpec((1, hd), lambda s, h: (0, h)),
        ],
        out_specs=tuple(out_specs),
        scratch_shapes=[pltpu.VMEM((seq_len, hd), F32), pltpu.VMEM((hd, hd), F32),
                        pltpu.VMEM((1, hd), F32), pltpu.VMEM((1, LANES), F32)],
        compiler_params=_params("arbitrary", "arbitrary"),
        name="mlstm",
    )(proj, proj, proj, proj, g_cols, g_rows, c0, n0, m0, norm_w)


def _outproj_body(rg_ref, ml_ref, w_ref, x_ref, mod_ref, lng_ref, lnb_ref, rw_ref, rb_ref,
                  x1_ref, u2p_ref, ti_ref, tg_ref, *, alpha, n_experts):
    mix = (jnp.dot(rg_ref[...], w_ref[0:D_RG, :], preferred_element_type=F32)
           + jnp.dot(ml_ref[...], w_ref[D_RG:D_MODEL, :], preferred_element_type=F32))
    z = alpha * x_ref[...] + mod_ref[2:3, :] * mix
    x1 = _layer_norm(z, lng_ref[...], lnb_ref[...])
    x1_ref[...] = x1
    u2 = x1 * (1.0 + mod_ref[4:5, :]) + mod_ref[3:4, :]
    tm = u2.shape[0]
    for s in range(TOKEN_TILE_ROWS):
        lo = u2[:, s * LANES:(s + 1) * LANES]
        hi = u2[:, D_MODEL // 2 + s * LANES:D_MODEL // 2 + (s + 1) * LANES]
        u2p_ref[pl.ds(s, tm, stride=TOKEN_TILE_ROWS), :] = _pack_bf16_pair(lo, hi)
    logits = jnp.dot(u2, rw_ref[...], preferred_element_type=F32, precision=HIGHEST) + rb_ref[...]
    lane = lax.broadcasted_iota(jnp.int32, logits.shape, 1)
    logits = jnp.where(lane < n_experts, logits, -jnp.inf)
    vals, idxs = [], []
    for _ in range(TOP_K):
        m = jnp.max(logits, axis=1, keepdims=True)
        idx = jnp.min(jnp.where(logits == m, lane, LANES), axis=1, keepdims=True)
        vals.append(m)
        idxs.append(idx)
        logits = jnp.where(lane == idx, -jnp.inf, logits)
    exps = [jnp.exp(v - vals[0]) for v in vals]
    inv = 1.0 / (exps[0] + exps[1] + exps[2] + exps[3])
    ti = jnp.zeros(lane.shape, jnp.int32)
    tg = jnp.zeros(lane.shape, F32)
    for k in range(TOP_K):
        ti = jnp.where(lane == k, idxs[k], ti)
        tg = jnp.where(lane == k, exps[k] * inv, tg)
    ti_ref[...] = ti
    tg_ref[...] = tg


def _outproj(rg_y, ml_y, w_out_bf, x, mod_l, ln_g, ln_b, rw_pad, rb_pad, alpha, n_experts, n_p, ds, tm=512):
    n, d = x.shape
    body = functools.partial(_outproj_body, alpha=alpha, n_experts=n_experts)
    row = lambda i: (i, 0)
    const = lambda i: (0, 0)
    return pl.pallas_call(
        body,
        out_shape=(jax.ShapeDtypeStruct((n, d), F32),
                   jax.ShapeDtypeStruct((n * TOKEN_TILE_ROWS, LANES), jnp.uint32),
                   jax.ShapeDtypeStruct((n, LANES), jnp.int32), jax.ShapeDtypeStruct((n, LANES), F32)),
        grid=(n // tm,),
        in_specs=[
            pl.BlockSpec((tm, D_RG), row),
            pl.BlockSpec((tm, D_ML), row),
            pl.BlockSpec((d, d), const),
            pl.BlockSpec((tm, d), row),
            pl.BlockSpec((None, N_MOD, d), lambda i: (_cond_index(i * tm, n_p, ds), 0, 0)),
            pl.BlockSpec((1, d), const),
            pl.BlockSpec((1, d), const),
            pl.BlockSpec((d, LANES), const),
            pl.BlockSpec((1, LANES), const),
        ],
        out_specs=(pl.BlockSpec((tm, d), row), pl.BlockSpec((tm * TOKEN_TILE_ROWS, LANES), row),
                   pl.BlockSpec((tm, LANES), row), pl.BlockSpec((tm, LANES), row)),
        compiler_params=_params("arbitrary"),
        name="outproj",
    )(rg_y, ml_y, w_out_bf, x, mod_l, ln_g, ln_b, rw_pad, rb_pad)


def _moe_body(order_ref, item_e_ref, item_start_ref, item_n_ref,
              u2p_hbm, w1g_ref, w1l_ref, w2_ref, b1g_ref, b1l_ref, b2_ref,
              y4p_hbm,
              xq_scr, xb_scr, acc_scr, ys_scr, w1g_scr, w1l_scr, w2_scr, gsem, ssem,
              *, n_assign, n_tokens, n_ft):
    s = pl.program_id(0)
    j = pl.program_id(1)
    n_rows = item_n_ref[s]
    start = item_start_ref[s]
    n_blocks = (n_rows + MOE_RB - 1) // MOE_RB
    d = acc_scr.shape[1]
    half = d // 2
    tr = TOKEN_TILE_ROWS

    def tile(ref, idx):
        return ref.at[pl.ds(pl.multiple_of(idx * tr, tr), tr)]

    def gather_copy(r):
        o = order_ref[jnp.minimum(start + r, n_assign - 1)]
        return pltpu.make_async_copy(tile(u2p_hbm, lax.shift_right_logical(o, 2)), tile(xq_scr, r), gsem)

    def scatter_copy(r):
        o = order_ref[start + r]
        dst = (o & (TOP_K - 1)) * n_tokens + lax.shift_right_logical(o, 2)
        return pltpu.make_async_copy(tile(ys_scr, r), tile(y4p_hbm, dst), ssem)

    def for_rows(count, fn):
        def group(g, c):
            for i in range(MOE_DMA_UNROLL):
                fn(g * MOE_DMA_UNROLL + i)
            return c
        n_groups = count // MOE_DMA_UNROLL
        lax.fori_loop(0, n_groups, group, 0)

        def single(r, c):
            fn(r)
            return c
        lax.fori_loop(n_groups * MOE_DMA_UNROLL, count, single, 0)

    def compute(row0, m):
        rows = pl.ds(row0, m)
        x = xb_scr[rows, :]
        hg = jnp.dot(x, w1g_scr[...], preferred_element_type=F32) + b1g_ref[...]
        hl = jnp.dot(x, w1l_scr[...], preferred_element_type=F32) + b1l_ref[...]
        hg = jnp.minimum(hg, SWIGLU_LIMIT)
        hl = jnp.clip(hl, -SWIGLU_LIMIT, SWIGLU_LIMIT)
        act = hg * _sigmoid(SWIGLU_ALPHA * hg) * (hl + 1.0)
        acc_scr[rows, :] += jnp.dot(act.astype(BF16), w2_scr[...], preferred_element_type=F32)

    @pl.when(n_rows > 0)
    def _item():
        @pl.when(j == 0)
        def _load_rows():
            for_rows(n_blocks * MOE_RB, lambda r: gather_copy(r).start())
            for_rows(n_blocks * MOE_RB, lambda r: gather_copy(r).wait())

            def prep(b, c):
                rows = pl.ds(pl.multiple_of(b * MOE_RB, MOE_RB), MOE_RB)
                for t in range(tr):
                    lo, hi = _unpack_bf16_pair(xq_scr[pl.ds(b * (MOE_RB * tr) + t, MOE_RB, stride=tr), :])
                    xb_scr[rows, t * LANES:(t + 1) * LANES] = lo.astype(BF16)
                    xb_scr[rows, half + t * LANES:half + (t + 1) * LANES] = hi.astype(BF16)
                acc_scr[rows, :] = jnp.broadcast_to(b2_ref[...], (MOE_RB, d))
                return c
            lax.fori_loop(0, n_blocks, prep, 0)

        w1g_scr[...] = w1g_ref[...].astype(BF16)
        w1l_scr[...] = w1l_ref[...].astype(BF16)
        w2_scr[...] = w2_ref[...].astype(BF16)

        n_pairs = n_blocks // 2

        def pair(p, c):
            compute(pl.multiple_of(p * (2 * MOE_RB), 2 * MOE_RB), 2 * MOE_RB)
            return c
        lax.fori_loop(0, n_pairs, pair, 0)

        @pl.when(n_blocks % 2 == 1)
        def _odd_block():
            compute(pl.multiple_of(n_pairs * (2 * MOE_RB), MOE_RB), MOE_RB)

        @pl.when(j == n_ft - 1)
        def _store_rows():
            def pack(b, c):
                rows = pl.ds(pl.multiple_of(b * MOE_RB, MOE_RB), MOE_RB)
                for t in range(tr):
                    lo = acc_scr[rows, t * LANES:(t + 1) * LANES]
                    hi = acc_scr[rows, half + t * LANES:half + (t + 1) * LANES]
                    ys_scr[pl.ds(b * (MOE_RB * tr) + t, MOE_RB, stride=tr), :] = _pack_bf16_pair(lo, hi)
                return c
            lax.fori_loop(0, n_blocks, pack, 0)
            for_rows(n_rows, lambda r: scatter_copy(r).start())
            for_rows(n_rows, lambda r: scatter_copy(r).wait())


def _moe_items(top_i, n_experts, max_items):
    n_assign = top_i.size
    e_flat = top_i.reshape(n_assign)
    onehot = (e_flat[:, None] == jnp.arange(n_experts, dtype=jnp.int32)[None, :]).astype(jnp.int32)
    csum = jnp.cumsum(onehot, axis=0)
    counts = csum[-1]
    starts = jnp.cumsum(counts) - counts
    dest = jnp.sum(onehot * (csum - 1 + starts[None, :]), axis=1)
    order = jnp.zeros((n_assign,), jnp.int32).at[dest].set(jnp.arange(n_assign, dtype=jnp.int32))
    items_per_e = (counts + MOE_CAP - 1) // MOE_CAP
    item_ends = jnp.cumsum(items_per_e)
    total = item_ends[-1]
    slot = jnp.arange(max_items, dtype=jnp.int32)
    valid = slot < total
    slot_c = jnp.minimum(slot, total - 1)
    e_s = jnp.minimum(jnp.sum((slot_c[:, None] >= item_ends[None, :]).astype(jnp.int32), axis=1), n_experts - 1)
    local = slot_c - (item_ends[e_s] - items_per_e[e_s])
    start = starts[e_s] + local * MOE_CAP
    n_rows = jnp.clip(counts[e_s] - local * MOE_CAP, 0, MOE_CAP)
    return (order, e_s, jnp.where(valid, start, 0).astype(jnp.int32),
            jnp.where(valid, n_rows, 0).astype(jnp.int32))


def _moe(u2p, top_i, layer, w1, b1, w2, b2):
    n = u2p.shape[0] // TOKEN_TILE_ROWS
    _, n_experts, d, two_de = w1.shape
    de = two_de // 2
    n_ft = de // MOE_TF
    n_assign = n * TOP_K
    max_items = n_experts + -(-n_assign // MOE_CAP)
    order, item_e, item_start, item_n = _moe_items(top_i, n_experts, max_items)
    body = functools.partial(_moe_body, n_assign=n_assign, n_tokens=n, n_ft=n_ft)

    def jt(s, j, item_n_ref):
        return jnp.where(item_n_ref[s] > 0, j, n_ft - 1)

    grid_spec = pltpu.PrefetchScalarGridSpec(
        num_scalar_prefetch=4,
        grid=(max_items, n_ft),
        in_specs=[
            pl.BlockSpec(memory_space=pl.ANY),
            pl.BlockSpec((None, None, d, MOE_TF), lambda s, j, o, ie, ist, inn: (layer, ie[s], 0, jt(s, j, inn))),
            pl.BlockSpec((None, None, d, MOE_TF),
                         lambda s, j, o, ie, ist, inn: (layer, ie[s], 0, n_ft + jt(s, j, inn))),
            pl.BlockSpec((None, None, MOE_TF, d), lambda s, j, o, ie, ist, inn: (layer, ie[s], jt(s, j, inn), 0)),
            pl.BlockSpec((None, None, 1, MOE_TF), lambda s, j, o, ie, ist, inn: (layer, ie[s], 0, jt(s, j, inn))),
            pl.BlockSpec((None, None, 1, MOE_TF),
                         lambda s, j, o, ie, ist, inn: (layer, ie[s], 0, n_ft + jt(s, j, inn))),
            pl.BlockSpec((None, None, 1, d), lambda s, j, o, ie, ist, inn: (layer, ie[s], 0, 0)),
        ],
        out_specs=pl.BlockSpec(memory_space=pl.ANY),
        scratch_shapes=[
            pltpu.VMEM((MOE_CAP * TOKEN_TILE_ROWS, LANES), jnp.uint32), pltpu.VMEM((MOE_CAP, d), BF16),
            pltpu.VMEM((MOE_CAP, d), F32), pltpu.VMEM((MOE_CAP * TOKEN_TILE_ROWS, LANES), jnp.uint32),
            pltpu.VMEM((d, MOE_TF), BF16), pltpu.VMEM((d, MOE_TF), BF16), pltpu.VMEM((MOE_TF, d), BF16),
            pltpu.SemaphoreType.DMA, pltpu.SemaphoreType.DMA,
        ],
    )
    b1r = b1.reshape(b1.shape[0], n_experts, 1, two_de)
    return pl.pallas_call(
        body,
        out_shape=jax.ShapeDtypeStruct((n_assign * TOKEN_TILE_ROWS, LANES), jnp.uint32),
        grid_spec=grid_spec,
        compiler_params=_params("arbitrary", "arbitrary"),
        name="moe",
    )(order, item_e, item_start, item_n, u2p, w1, w1, w2, b1r, b1r, b2.reshape(b2.shape[0], n_experts, 1, d))


def _combine_body(x1_ref, y0_ref, y1_ref, y2_ref, y3_ref, tg_ref, mod_ref, modn_ref, lng_ref, lnb_ref,
                  x2_ref, un_ref, ff_scr, *, alpha):
    tm, d = x1_ref.shape
    half = d // 2
    tr = TOKEN_TILE_ROWS
    tg = tg_ref[...]
    for t in range(tr):
        lo_sum = hi_sum = None
        for k, y_ref in enumerate((y0_ref, y1_ref, y2_ref, y3_ref)):
            lo, hi = _unpack_bf16_pair(y_ref[pl.ds(t, tm, stride=tr), :])
            g = tg[:, k:k + 1]
            lo_sum = g * lo if lo_sum is None else lo_sum + g * lo
            hi_sum = g * hi if hi_sum is None else hi_sum + g * hi
        ff_scr[:, t * LANES:(t + 1) * LANES] = lo_sum
        ff_scr[:, half + t * LANES:half + (t + 1) * LANES] = hi_sum
    z = alpha * x1_ref[...] + mod_ref[5:6, :] * ff_scr[...]
    x2 = _layer_norm(z, lng_ref[...], lnb_ref[...])
    x2_ref[...] = x2
    un_ref[...] = (x2 * (1.0 + modn_ref[1:2, :]) + modn_ref[0:1, :]).astype(BF16)


def _combine(x1, y4p, tg, mod_l, mod_next, ln_g, ln_b, alpha, n_p, ds, tm=256):
    n, d = x1.shape
    body = functools.partial(_combine_body, alpha=alpha)
    row = lambda i: (i, 0)
    const = lambda i: (0, 0)
    cond = lambda i: (_cond_index(i * tm, n_p, ds), 0, 0)
    y_specs = [pl.BlockSpec((tm * TOKEN_TILE_ROWS, LANES), lambda i, k=k: (k * (n // tm) + i, 0))
               for k in range(TOP_K)]
    return pl.pallas_call(
        body,
        out_shape=(jax.ShapeDtypeStruct((n, d), F32), jax.ShapeDtypeStruct((n, d), BF16)),
        grid=(n // tm,),
        in_specs=[
            pl.BlockSpec((tm, d), row),
            *y_specs,
            pl.BlockSpec((tm, LANES), row),
            pl.BlockSpec((None, N_MOD, d), cond),
            pl.BlockSpec((None, N_MOD, d), cond),
            pl.BlockSpec((1, d), const),
            pl.BlockSpec((1, d), const),
        ],
        out_specs=(pl.BlockSpec((tm, d), row), pl.BlockSpec((tm, d), row)),
        scratch_shapes=[pltpu.VMEM((tm, d), F32)],
        compiler_params=_params("arbitrary"),
        name="combine",
    )(x1, y4p, y4p, y4p, y4p, tg, mod_l, mod_next, ln_g, ln_b)


def _block_diag_tiles(w, tc):
    depth, nb, bw, _ = w.shape
    per = tc // bw
    wt = w.reshape(depth, nb // per, per, bw, bw)
    eye = jnp.eye(per, dtype=w.dtype)
    full = jnp.einsum("lgajk,ab->lgajbk", wt, eye)
    return full.reshape(depth, nb // per, tc, tc)


def _rg_gate_weights(rg_wa, rg_ba, rg_wx, rg_bx, tc=256):
    depth = rg_wa.shape[0]
    n_ct = D_RG // tc
    tiles = [_block_diag_tiles(w[:, d], tc) for d in (0, 1) for w in (rg_wa, rg_wx)]
    wg = jnp.concatenate(tiles, axis=-1).astype(BF16)
    biases = [b[:, d].reshape(depth, n_ct, 1, tc) for d in (0, 1) for b in (rg_ba, rg_bx)]
    bg = jnp.concatenate(biases, axis=-1)
    return wg, bg


def kernel(x_prompt, x_sample, state_rglru, state_mlstm_C, state_mlstm_n, state_mlstm_m, c, c_ctx, w_ada, b_ada, w_in, ml_gate_b, rg_conv_w, rg_conv_b, rg_wa, rg_ba, rg_wx, rg_bx, rg_lambda, ml_norm_w, w_out, ln1_g, ln1_b, router_w, router_b, moe_w1, moe_b1, moe_w2, moe_b2, ln2_g, ln2_b):
    depth = w_ada.shape[0]
    bsz, seq, d = x_prompt.shape
    dbs, dseq, _ = x_sample.shape
    n_p, n_s = bsz * seq, dbs * dseq
    n_experts = router_w.shape[-1]
    alpha = float((2 * depth) ** 0.25)
    assert d == D_MODEL and seq == CHUNK and dseq % CHUNK == 0 and n_p % dseq == 0

    x = jnp.concatenate([x_prompt.reshape(n_p, d), x_sample.reshape(n_s, d)], axis=0)
    n_cond = 1 + dbs
    conds = jnp.concatenate([c_ctx[None, :], c, jnp.zeros((SUBLANES - n_cond, d), F32)], axis=0)
    mod = _ada(conds, w_ada, b_ada)
    mod = mod[:, :n_cond].reshape(depth, n_cond, N_MOD, d)

    w_in_bf = w_in.astype(BF16)
    w_g = w_in[:, :, D_PROJ:]
    wg_pad = jnp.pad(w_g, ((0, 0), (0, 0), (0, LANES - N_GATES))).astype(BF16)
    wgt = jnp.swapaxes(w_g, 1, 2).astype(BF16)
    gb_row = jnp.pad(ml_gate_b, ((0, 0), (0, LANES - N_GATES))).reshape(depth, 1, LANES)
    gb_col = ml_gate_b.reshape(depth, N_GATES, 1)
    rg_wg, rg_bg = _rg_gate_weights(rg_wa, rg_ba, rg_wx, rg_bx)
    w_out_bf = w_out.astype(BF16)
    rw_pad = jnp.pad(router_w, ((0, 0), (0, 0), (0, LANES - n_experts)))
    rb_pad = jnp.pad(router_b, ((0, 0), (0, LANES - n_experts))).reshape(depth, 1, LANES)

    zero_rg = jnp.zeros((1, 2, D_RG), F32)
    zero_c = jnp.zeros((1, 2, ML_HEADS, ML_HD, ML_HD), F32)
    zero_n = jnp.zeros((1, 2, ML_HEADS, 1, ML_HD), F32)
    zero_m = jnp.zeros((1, 2, ML_HEADS, 1, LANES), F32)

    u = _modulate(x, mod[0], n_p, dseq)
    states = []
    for l in range(depth):
        proj = _inproj(u, w_in_bf[l])
        g, gt = _gates(u, wg_pad[l], wgt[l], gb_row[l], gb_col[l])
        g_cols = g[:, :N_GATES].reshape(n_p + n_s, 4, ML_HEADS).transpose(2, 0, 1)
        g_rows = gt.reshape(4, ML_HEADS, n_p + n_s).transpose(1, 0, 2)

        conv_b = rg_conv_b[l].reshape(1, D_RG)
        rg_p, ht_p = _rglru(proj, 0, bsz, seq, seq, rg_conv_w[l], conv_b, rg_wg[l], rg_bg[l], rg_lambda[l], zero_rg)
        rg_s, _ = _rglru(proj, n_p // dseq, dbs, dseq, GRID_W, rg_conv_w[l], conv_b, rg_wg[l], rg_bg[l],
                         rg_lambda[l], state_rglru[:, l])

        norm_w = ml_norm_w[l].reshape(1, D_ML)
        ml_p, c_p, nn_p, m_p = _mlstm(proj, 0, bsz, seq, g_cols, g_rows, zero_c, zero_n, zero_m, norm_w, True)
        m0_s = jnp.broadcast_to(state_mlstm_m[:, l][..., None, None], (dbs, 2, ML_HEADS, 1, LANES))
        (ml_s,) = _mlstm(proj, n_p // dseq, dbs, dseq, g_cols, g_rows, state_mlstm_C[:, l],
                         state_mlstm_n[:, l][:, :, :, None, :], m0_s, norm_w, False)
        states.append((ht_p, c_p, nn_p[:, :, :, 0, :], m_p[:, :, :, 0, 0]))

        rg_y = jnp.concatenate([rg_p, rg_s], axis=0)
        ml_y = jnp.concatenate([ml_p, ml_s], axis=0)
        x1, u2p, ti, tg = _outproj(rg_y, ml_y, w_out_bf[l], x, mod[l], ln1_g[l].reshape(1, d), ln1_b[l].reshape(1, d),
                                  rw_pad[l], rb_pad[l], alpha, n_experts, n_p, dseq)
        y4p = _moe(u2p, ti[:, :TOP_K], l, moe_w1, moe_b1, moe_w2, moe_b2)
        x, u = _combine(x1, y4p, tg, mod[l], mod[min(l + 1, depth - 1)], ln2_g[l].reshape(1, d),
                        ln2_b[l].reshape(1, d), alpha, n_p, dseq)

    new_rglru = jnp.stack([s[0] for s in states], axis=1)
    new_c = jnp.stack([s[1] for s in states], axis=1)
    new_n = jnp.stack([s[2] for s in states], axis=1)
    new_m = jnp.stack([s[3] for s in states], axis=1)
    return (x[:n_p].reshape(bsz, seq, d), x[n_p:].reshape(dbs, dseq, d), new_rglru, new_c, new_n, new_m)
```

```python
import functools

import jax
import jax.numpy as jnp
from jax import lax
from jax.experimental import pallas as pl
from jax.experimental.pallas import tpu as pltpu

F32 = jnp.float32
BF16 = jnp.bfloat16

D_MODEL = 2048
D_RG = 1024
RG_BLOCK_W = 64
RG_CONV_W = 4
RG_C = 8.0
GRID_W = 64
D_ML = 1024
ML_HEADS = 4
ML_HD = 256
TOP_K = 4
SWIGLU_LIMIT = 7.0
SWIGLU_ALPHA = 1.702
LN_EPS = 1e-5
N_MOD = 6
CHUNK = 256
SUBLANES = 8
LANES = 128
N_GATES = 4 * ML_HEADS
COL_Q = 2 * D_RG
COL_K = COL_Q + D_ML
COL_V = COL_K + D_ML
COL_O = COL_V + D_ML
D_PROJ = COL_O + D_ML

TOKEN_TILE_ROWS = D_MODEL // 2 // LANES
MOE_CAP = 1280
MOE_RB = 128
MOE_TF = 256
MOE_DMA_UNROLL = 16
VMEM_LIMIT = 56 * 1024 * 1024
HIGHEST = lax.Precision.HIGHEST


def _pack_bf16_pair(lo, hi):
    lo_bits = lax.bitcast_convert_type(lo.astype(BF16).astype(F32), jnp.uint32)
    hi_bits = lax.bitcast_convert_type(hi.astype(BF16).astype(F32), jnp.uint32)
    return (lo_bits >> 16) | (hi_bits & jnp.uint32(0xFFFF0000))


def _unpack_bf16_pair(w):
    lo = lax.bitcast_convert_type(w << 16, F32)
    hi = lax.bitcast_convert_type(w & jnp.uint32(0xFFFF0000), F32)
    return lo, hi


def _sigmoid(x):
    return 1.0 / (1.0 + jnp.exp(-x))


def _softplus(x):
    return jnp.maximum(x, 0.0) + jnp.log1p(jnp.exp(-jnp.abs(x)))


def _gelu_tanh(x):
    return 0.5 * x * (1.0 + jnp.tanh(0.7978845608028654 * (x + 0.044715 * (x * x * x))))


def _layer_norm(z, g, b):
    mu = jnp.mean(z, axis=-1, keepdims=True)
    zc = z - mu
    var = jnp.mean(zc * zc, axis=-1, keepdims=True)
    return zc * lax.rsqrt(var + LN_EPS) * g + b


def _params(*sem):
    return pltpu.CompilerParams(dimension_semantics=sem, vmem_limit_bytes=VMEM_LIMIT)


def _cond_index(row_start, n_p, ds):
    return jnp.where(row_start < n_p, 0, 1 + (row_start - n_p) // ds)


def _ada_body(c_ref, w_ref, b_ref, o_ref):
    c = c_ref[...]
    s = c * _sigmoid(c)
    o_ref[...] = jnp.dot(s, w_ref[...], preferred_element_type=F32, precision=HIGHEST) + b_ref[...]


def _ada(conds, w_ada, b_ada, tn=1024):
    depth, d, n6 = w_ada.shape
    nc = conds.shape[0]
    return pl.pallas_call(
        _ada_body,
        out_shape=jax.ShapeDtypeStruct((depth, nc, n6), F32),
        grid=(depth, n6 // tn),
        in_specs=[
            pl.BlockSpec((nc, d), lambda l, j: (0, 0)),
            pl.BlockSpec((None, d, tn), lambda l, j: (l, 0, j)),
            pl.BlockSpec((None, 1, tn), lambda l, j: (l, 0, j)),
        ],
        out_specs=pl.BlockSpec((None, nc, tn), lambda l, j: (l, 0, j)),
        compiler_params=_params("arbitrary", "arbitrary"),
        name="ada",
    )(conds, w_ada, b_ada.reshape(depth, 1, n6))


def _modulate_body(x_ref, mod_ref, u_ref):
    u_ref[...] = (x_ref[...] * (1.0 + mod_ref[1:2, :]) + mod_ref[0:1, :]).astype(BF16)


def _modulate(x, mod_l, n_p, ds, tm=512):
    n, d = x.shape
    return pl.pallas_call(
        _modulate_body,
        out_shape=jax.ShapeDtypeStruct((n, d), BF16),
        grid=(n // tm,),
        in_specs=[
            pl.BlockSpec((tm, d), lambda i: (i, 0)),
            pl.BlockSpec((None, N_MOD, d), lambda i: (_cond_index(i * tm, n_p, ds), 0, 0)),
        ],
        out_specs=pl.BlockSpec((tm, d), lambda i: (i, 0)),
        compiler_params=_params("arbitrary"),
        name="modulate",
    )(x, mod_l)


def _inproj_body(u_ref, w_ref, o_ref):
    o_ref[...] = jnp.dot(u_ref[...], w_ref[...], preferred_element_type=F32)


def _inproj(u, w_bf, tm=1024, tn=768):
    n, d = u.shape
    return pl.pallas_call(
        _inproj_body,
        out_shape=jax.ShapeDtypeStruct((n, D_PROJ), F32),
        grid=(D_PROJ // tn, n // tm),
        in_specs=[
            pl.BlockSpec((tm, d), lambda j, i: (i, 0)),
            pl.BlockSpec((d, tn), lambda j, i: (0, j)),
        ],
        out_specs=pl.BlockSpec((tm, tn), lambda j, i: (i, j)),
        compiler_params=_params("arbitrary", "arbitrary"),
        name="inproj",
    )(u, w_bf)


def _gates_body(u_ref, wg_ref, wgt_ref, brow_ref, bcol_ref, g_ref, gt_ref):
    u = u_ref[...]
    g = jnp.dot(u, wg_ref[...], preferred_element_type=F32) + brow_ref[...]
    lane = lax.broadcasted_iota(jnp.int32, g.shape, 1)
    is_forget = ((lane >> 2) & 1) == 1
    g_ref[...] = jnp.where(is_forget, -_softplus(-g), g)
    gt = lax.dot_general(wgt_ref[...], u, (((1,), (1,)), ((), ())), preferred_element_type=F32) + bcol_ref[...]
    row = lax.broadcasted_iota(jnp.int32, gt.shape, 0)
    is_forget_t = ((row >> 2) & 1) == 1
    gt_ref[...] = jnp.where(is_forget_t, -_softplus(-gt), gt)


def _gates(u, wg_pad, wgt, brow, bcol, tm=1024):
    n, d = u.shape
    return pl.pallas_call(
        _gates_body,
        out_shape=(jax.ShapeDtypeStruct((n, LANES), F32), jax.ShapeDtypeStruct((N_GATES, n), F32)),
        grid=(n // tm,),
        in_specs=[
            pl.BlockSpec((tm, d), lambda i: (i, 0)),
            pl.BlockSpec((d, LANES), lambda i: (0, 0)),
            pl.BlockSpec((N_GATES, d), lambda i: (0, 0)),
            pl.BlockSpec((1, LANES), lambda i: (0, 0)),
            pl.BlockSpec((N_GATES, 1), lambda i: (0, 0)),
        ],
        out_specs=(pl.BlockSpec((tm, LANES), lambda i: (i, 0)), pl.BlockSpec((N_GATES, tm), lambda i: (0, i))),
        compiler_params=_params("arbitrary"),
        name="gates",
    )(u, wg_pad, wgt, brow, bcol)


def _rglru_body(rx_ref, rgate_ref, cw_ref, cb_ref, wg_ref, bg_ref, lam_ref, h0_ref, y_ref, ht_ref,
                af_scr, uf_scr, ab_scr, ub_scr, hf_scr, *, seq_len, period):
    tc = rx_ref.shape[1]
    n_chunks = seq_len // CHUNK
    n_tiles = seq_len // SUBLANES
    sp = _softplus(-lam_ref[...])

    def gate_chunk(c, carry):
        rows = pl.ds(pl.multiple_of(c * CHUNK, CHUNK), CHUNK)
        x = rx_ref[rows, :]
        t = lax.broadcasted_iota(jnp.int32, (CHUNK, 1), 0) & (period - 1)
        xm2 = jnp.where(t >= 2, pltpu.roll(x, 2, 0), 0.0)
        xm1 = jnp.where(t >= 1, pltpu.roll(x, 1, 0), 0.0)
        xp1 = jnp.where(t <= period - 2, pltpu.roll(x, CHUNK - 1, 0), 0.0)
        xc = (cw_ref[0:1, :] * xm2 + cw_ref[1:2, :] * xm1 + cw_ref[2:3, :] * x + cw_ref[3:4, :] * xp1
              + cb_ref[...])
        g = jnp.dot(xc.astype(BF16), wg_ref[...], preferred_element_type=F32) + bg_ref[...]
        for d, (a_scr, u_scr) in enumerate(((af_scr, uf_scr), (ab_scr, ub_scr))):
            r = _sigmoid(g[:, (2 * d) * tc:(2 * d + 1) * tc])
            i = _sigmoid(g[:, (2 * d + 1) * tc:(2 * d + 2) * tc])
            log_a = (-RG_C) * r * sp[d:d + 1, :]
            a = jnp.exp(log_a)
            a_scr[rows, :] = a
            u_scr[rows, :] = jnp.sqrt(-jnp.tanh(log_a) * (a * a + 1.0)) * (i * xc)
        return carry

    lax.fori_loop(0, n_chunks, gate_chunk, 0)

    sub = lax.broadcasted_iota(jnp.int32, (SUBLANES, 1), 0)

    def tile_scan(a, u, reverse):
        for d in (1, 2, 4):
            if reverse:
                valid = sub < SUBLANES - d
                shift = SUBLANES - d
            else:
                valid = sub >= d
                shift = d
            a_sh = pltpu.roll(a, shift, 0)
            u_sh = pltpu.roll(u, shift, 0)
            u = jnp.where(valid, a * u_sh + u, u)
            a = jnp.where(valid, a * a_sh, a)
        return a, u

    def fwd(i, h_prev):
        rows = pl.ds(pl.multiple_of(i * SUBLANES, SUBLANES), SUBLANES)
        a, u = tile_scan(af_scr[rows, :], uf_scr[rows, :], False)
        h = a * h_prev + u
        hf_scr[rows, :] = h
        return jnp.broadcast_to(h[SUBLANES - 1:SUBLANES, :], h.shape)

    h0f = jnp.broadcast_to(h0_ref[0:1, :], (SUBLANES, tc))
    h_last = lax.fori_loop(0, n_tiles, fwd, h0f)
    ht_ref[0:1, :] = h_last[0:1, :]

    def bwd(k, h_next):
        i = n_tiles - 1 - k
        rows = pl.ds(pl.multiple_of(i * SUBLANES, SUBLANES), SUBLANES)
        a, u = tile_scan(ab_scr[rows, :], ub_scr[rows, :], True)
        h = a * h_next + u
        y = (hf_scr[rows, :] + h) * _gelu_tanh(rgate_ref[rows, :])
        y_ref[rows, :] = y.astype(y_ref.dtype)
        return jnp.broadcast_to(h[0:1, :], h.shape)

    h0b = jnp.broadcast_to(h0_ref[1:2, :], (SUBLANES, tc))
    h_first = lax.fori_loop(0, n_tiles, bwd, h0b)
    ht_ref[1:2, :] = h_first[0:1, :]


def _rglru(proj, row_block0, n_seq, seq_len, period, conv_w, conv_b, wg, bg, lam, h0, tc=256):
    n_ct = D_RG // tc
    body = functools.partial(_rglru_body, seq_len=seq_len, period=period)
    h0_map = (lambda s, g: (s, 0, g)) if h0.shape[0] > 1 else (lambda s, g: (0, 0, g))
    return pl.pallas_call(
        body,
        out_shape=(jax.ShapeDtypeStruct((n_seq * seq_len, D_RG), BF16),
                   jax.ShapeDtypeStruct((n_seq, 2, D_RG), F32)),
        grid=(n_seq, n_ct),
        in_specs=[
            pl.BlockSpec((seq_len, tc), lambda s, g: (row_block0 + s, g)),
            pl.BlockSpec((seq_len, tc), lambda s, g: (row_block0 + s, n_ct + g)),
            pl.BlockSpec((RG_CONV_W, tc), lambda s, g: (0, g)),
            pl.BlockSpec((1, tc), lambda s, g: (0, g)),
            pl.BlockSpec((None, tc, 4 * tc), lambda s, g: (g, 0, 0)),
            pl.BlockSpec((None, 1, 4 * tc), lambda s, g: (g, 0, 0)),
            pl.BlockSpec((2, tc), lambda s, g: (0, g)),
            pl.BlockSpec((None, 2, tc), h0_map),
        ],
        out_specs=(pl.BlockSpec((seq_len, tc), lambda s, g: (s, g)),
                   pl.BlockSpec((None, 2, tc), lambda s, g: (s, 0, g))),
        scratch_shapes=[pltpu.VMEM((seq_len, tc), F32) for _ in range(5)],
        compiler_params=_params("arbitrary", "arbitrary"),
        name="rglru",
    )(proj, proj, conv_w, conv_b, wg, bg, lam, h0)


def _mlstm_body(q_ref, k_ref, v_ref, og_ref, gc_ref, gr_ref, c0_ref, n0_ref, m0_ref, nw_ref,
                y_ref, *rest, seq_len, with_state):
    if with_state:
        c_out, n_out, m_out, hf_scr, c_scr, n_scr, m_scr = rest
    else:
        hf_scr, c_scr, n_scr, m_scr = rest
    L = CHUNK
    n_chunks = seq_len // L
    ii = lax.broadcasted_iota(jnp.int32, (L, L), 0)
    jj = lax.broadcasted_iota(jnp.int32, (L, L), 1)
    scale = ML_HD ** -0.5

    def chunk_step(rows, direction, update_state):
        mask = (jj <= ii) if direction == 0 else (jj >= ii)
        mask_t = (ii <= jj) if direction == 0 else (ii >= jj)
        ig_r = gr_ref[2 * direction:2 * direction + 1, rows]
        lf_r = gr_ref[2 * direction + 1:2 * direction + 2, rows]
        ig_c = gc_ref[rows, 2 * direction:2 * direction + 1]
        lf_c = gc_ref[rows, 2 * direction + 1:2 * direction + 2]
        b_col = jnp.sum(jnp.where(mask, lf_r, 0.0), axis=1, keepdims=True)
        b_row = jnp.sum(jnp.where(mask_t, lf_c, 0.0), axis=0, keepdims=True)
        total = jnp.sum(lf_r, axis=1, keepdims=True)
        m_prev = m_scr[0:1, 0:1]
        dmat = jnp.where(mask, b_col - b_row + ig_r, -jnp.inf)
        inter = b_col + m_prev
        m_t = jnp.maximum(inter, jnp.max(dmat, axis=1, keepdims=True))
        w_inter = jnp.exp(inter - m_t)
        q = q_ref[rows, :]
        k = k_ref[rows, :] * scale
        v = v_ref[rows, :]
        qb, kb, vb = q.astype(BF16), k.astype(BF16), v.astype(BF16)
        qk = lax.dot_general(qb, kb, (((1,), (1,)), ((), ())), preferred_element_type=F32)
        s = qk * jnp.exp(dmat - m_t)
        c_prev = c_scr[...]
        n_prev = n_scr[...]
        num = (w_inter * jnp.dot(qb, c_prev.astype(BF16), preferred_element_type=F32)
               + jnp.dot(s.astype(BF16), vb, preferred_element_type=F32))
        den = w_inter * jnp.sum(q * n_prev, axis=1, keepdims=True) + jnp.sum(s, axis=1, keepdims=True)
        h = num / jnp.maximum(jnp.abs(den), jnp.exp(-m_t))
        if update_state:
            g_row = total - b_row + ig_r
            g_col = total - b_col + ig_c
            m_new = jnp.maximum(total + m_prev, jnp.max(g_row, axis=1, keepdims=True))
            decay = jnp.exp(total + m_prev - m_new)
            kw = k * jnp.exp(g_col - m_new)
            c_scr[...] = decay * c_prev + lax.dot_general(
                kw.astype(BF16), vb, (((0,), (0,)), ((), ())), preferred_element_type=F32)
            n_scr[...] = decay * n_prev + jnp.sum(kw, axis=0, keepdims=True)
            m_scr[...] = jnp.broadcast_to(m_new, m_scr.shape)
        return h

    for direction in (0, 1):
        c_scr[...] = c0_ref[direction]
        n_scr[...] = n0_ref[direction]
        m_scr[...] = m0_ref[direction]

        def body(step, carry, direction=direction):
            c = step if direction == 0 else n_chunks - 1 - step
            rows = pl.ds(pl.multiple_of(c * L, L), L)
            h = chunk_step(rows, direction, with_state or n_chunks > 1)
            if direction == 0:
                hf_scr[rows, :] = h
            else:
                hm = hf_scr[rows, :] + h
                mu = jnp.mean(hm, axis=1, keepdims=True)
                hc = hm - mu
                var = jnp.mean(hc * hc, axis=1, keepdims=True)
                hn = hc * lax.rsqrt(var + LN_EPS) * nw_ref[...]
                y_ref[rows, :] = (_sigmoid(og_ref[rows, :]) * hn).astype(y_ref.dtype)
            return carry

        lax.fori_loop(0, n_chunks, body, 0)
        if with_state:
            c_out[direction] = c_scr[...]
            n_out[direction] = n_scr[...]
            m_out[direction] = m_scr[...]


def _mlstm(proj, row_block0, n_seq, seq_len, g_cols, g_rows, c0, n0, m0, norm_w, with_state):
    hd = ML_HD
    body = functools.partial(_mlstm_body, seq_len=seq_len, with_state=with_state)
    bcast = c0.shape[0] == 1
    smap = (lambda s: 0) if bcast else (lambda s: s)
    col = lambda base: (lambda s, h: (row_block0 + s, base // hd + h))
    out_shape = [jax.ShapeDtypeStruct((n_seq * seq_len, D_ML), BF16)]
    out_specs = [pl.BlockSpec((seq_len, hd), lambda s, h: (s, h))]
    if with_state:
        out_shape += [jax.ShapeDtypeStruct((n_seq, 2, ML_HEADS, hd, hd), F32),
                      jax.ShapeDtypeStruct((n_seq, 2, ML_HEADS, 1, hd), F32),
                      jax.ShapeDtypeStruct((n_seq, 2, ML_HEADS, 1, LANES), F32)]
        out_specs += [pl.BlockSpec((None, 2, None, hd, hd), lambda s, h: (s, 0, h, 0, 0)),
                      pl.BlockSpec((None, 2, None, 1, hd), lambda s, h: (s, 0, h, 0, 0)),
                      pl.BlockSpec((None, 2, None, 1, LANES), lambda s, h: (s, 0, h, 0, 0))]
    return pl.pallas_call(
        body,
        out_shape=tuple(out_shape),
        grid=(n_seq, ML_HEADS),
        in_specs=[
            pl.BlockSpec((seq_len, hd), col(COL_Q)),
            pl.BlockSpec((seq_len, hd), col(COL_K)),
            pl.BlockSpec((seq_len, hd), col(COL_V)),
            pl.BlockSpec((seq_len, hd), col(COL_O)),
            pl.BlockSpec((None, seq_len, 4), lambda s, h: (h, row_block0 + s, 0)),
            pl.BlockSpec((None, 4, seq_len), lambda s, h: (h, 0, row_block0 + s)),
            pl.BlockSpec((None, 2, None, hd, hd), lambda s, h: (smap(s), 0, h, 0, 0)),
            pl.BlockSpec((None, 2, None, 1, hd), lambda s, h: (smap(s), 0, h, 0, 0)),
            pl.BlockSpec((None, 2, None, 1, LANES), lambda s, h: (smap(s), 0, h, 0, 0)),
            pl.BlockSpec((1, hd), lambda s, h: (0, h)),
        ],
        out_specs=tuple(out_specs),
        scratch_shapes=[pltpu.VMEM((seq_len, hd), F32), pltpu.VMEM((hd, hd), F32),
                        pltpu.VMEM((1, hd), F32), pltpu.VMEM((1, LANES), F32)],
        compiler_params=_params("arbitrary", "arbitrary"),
        name="mlstm",
    )(proj, proj, proj, proj, g_cols, g_rows, c0, n0, m0, norm_w)


def _outproj_body(rg_ref, ml_ref, w_ref, x_ref, mod_ref, lng_ref, lnb_ref, rw_ref, rb_ref,
                  x1_ref, u2p_ref, ti_ref, tg_ref, *, alpha, n_experts):
    mix = (jnp.dot(rg_ref[...], w_ref[0:D_RG, :], preferred_element_type=F32)
           + jnp.dot(ml_ref[...], w_ref[D_RG:D_MODEL, :], preferred_element_type=F32))
    z = alpha * x_ref[...] + mod_ref[2:3, :] * mix
    x1 = _layer_norm(z, lng_ref[...], lnb_ref[...])
    x1_ref[...] = x1
    u2 = x1 * (1.0 + mod_ref[4:5, :]) + mod_ref[3:4, :]
    tm = u2.shape[0]
    for s in range(TOKEN_TILE_ROWS):
        lo = u2[:, s * LANES:(s + 1) * LANES]
        hi = u2[:, D_MODEL // 2 + s * LANES:D_MODEL // 2 + (s + 1) * LANES]
        u2p_ref[pl.ds(s, tm, stride=TOKEN_TILE_ROWS), :] = _pack_bf16_pair(lo, hi)
    logits = jnp.dot(u2, rw_ref[...], preferred_element_type=F32, precision=HIGHEST) + rb_ref[...]
    lane = lax.broadcasted_iota(jnp.int32, logits.shape, 1)
    logits = jnp.where(lane < n_experts, logits, -jnp.inf)
    vals, idxs = [], []
    for _ in range(TOP_K):
        m = jnp.max(logits, axis=1, keepdims=True)
        idx = jnp.min(jnp.where(logits == m, lane, LANES), axis=1, keepdims=True)
        vals.append(m)
        idxs.append(idx)
        logits = jnp.where(lane == idx, -jnp.inf, logits)
    exps = [jnp.exp(v - vals[0]) for v in vals]
    inv = 1.0 / (exps[0] + exps[1] + exps[2] + exps[3])
    ti = jnp.zeros(lane.shape, jnp.int32)
    tg = jnp.zeros(lane.shape, F32)
    for k in range(TOP_K):
        ti = jnp.where(lane == k, idxs[k], ti)
        tg = jnp.where(lane == k, exps[k] * inv, tg)
    ti_ref[...] = ti
    tg_ref[...] = tg


def _outproj(rg_y, ml_y, w_out_bf, x, mod_l, ln_g, ln_b, rw_pad, rb_pad, alpha, n_experts, n_p, ds, tm=512):
    n, d = x.shape
    body = functools.partial(_outproj_body, alpha=alpha, n_experts=n_experts)
    row = lambda i: (i, 0)
    const = lambda i: (0, 0)
    return pl.pallas_call(
        body,
        out_shape=(jax.ShapeDtypeStruct((n, d), F32),
                   jax.ShapeDtypeStruct((n * TOKEN_TILE_ROWS, LANES), jnp.uint32),
                   jax.ShapeDtypeStruct((n, LANES), jnp.int32), jax.ShapeDtypeStruct((n, LANES), F32)),
        grid=(n // tm,),
        in_specs=[
            pl.BlockSpec((tm, D_RG), row),
            pl.BlockSpec((tm, D_ML), row),
            pl.BlockSpec((d, d), const),
            pl.BlockSpec((tm, d), row),
            pl.BlockSpec((None, N_MOD, d), lambda i: (_cond_index(i * tm, n_p, ds), 0, 0)),
            pl.BlockSpec((1, d), const),
            pl.BlockSpec((1, d), const),
            pl.BlockSpec((d, LANES), const),
            pl.BlockSpec((1, LANES), const),
        ],
        out_specs=(pl.BlockSpec((tm, d), row), pl.BlockSpec((tm * TOKEN_TILE_ROWS, LANES), row),
                   pl.BlockSpec((tm, LANES), row), pl.BlockSpec((tm, LANES), row)),
        compiler_params=_params("arbitrary"),
        name="outproj",
    )(rg_y, ml_y, w_out_bf, x, mod_l, ln_g, ln_b, rw_pad, rb_pad)


def _moe_items(top_i, n_experts, max_items):
    n_assign = top_i.size
    e_flat = top_i.reshape(n_assign)
    order = jnp.argsort(e_flat, stable=True).astype(jnp.int32)
    counts = jnp.sum((e_flat[:, None] == jnp.arange(n_experts, dtype=jnp.int32)[None, :]).astype(jnp.int32), axis=0)
    starts = jnp.cumsum(counts) - counts
    items_per_e = (counts + MOE_CAP - 1) // MOE_CAP
    item_ends = jnp.cumsum(items_per_e)
    total = item_ends[-1]
    slot = jnp.arange(max_items, dtype=jnp.int32)
    valid = slot < total
    slot_c = jnp.minimum(slot, total - 1)
    e_s = jnp.minimum(jnp.sum((slot_c[:, None] >= item_ends[None, :]).astype(jnp.int32), axis=1), n_experts - 1)
    local = slot_c - (item_ends[e_s] - items_per_e[e_s])
    start = starts[e_s] + local * MOE_CAP
    n_rows = jnp.clip(counts[e_s] - local * MOE_CAP, 0, MOE_CAP)
    return (order, e_s, jnp.where(valid, start, 0).astype(jnp.int32),
            jnp.where(valid, n_rows, 0).astype(jnp.int32), total.reshape(1).astype(jnp.int32))


def _moe_body(order_ref, item_e_ref, item_start_ref, item_n_ref, n_items_ref,
              u2p_hbm, w1_hbm, w2_hbm, b1_ref, b2_ref,
              y4p_hbm,
              xq_scr, xb_scr, acc_scr, ys_scr, w1g_f, w1l_f, w2_f, w1g_b, w1l_b, w2_b, gsem, ssem, wsem,
              *, layer, n_assign, n_tokens, n_ft):
    d = acc_scr.shape[1]
    half = d // 2
    de = n_ft * MOE_TF
    tr = TOKEN_TILE_ROWS
    n_items = n_items_ref[0]
    n_steps = n_items * n_ft

    def tile(ref, idx):
        return ref.at[pl.ds(pl.multiple_of(idx * tr, tr), tr)]

    def granules(n_rows):
        return (n_rows + MOE_RB - 1) // MOE_RB

    def gather_copy(it, r):
        o = order_ref[jnp.minimum(item_start_ref[it] + r, n_assign - 1)]
        return pltpu.make_async_copy(tile(u2p_hbm, lax.shift_right_logical(o, 2)), tile(xq_scr, r), gsem)

    def scatter_copy(it, r):
        o = order_ref[item_start_ref[it] + r]
        dst = (o & (TOP_K - 1)) * n_tokens + lax.shift_right_logical(o, 2)
        return pltpu.make_async_copy(tile(ys_scr, r), tile(y4p_hbm, dst), ssem)

    def for_rows(count, fn):
        def group(g, c):
            for i in range(MOE_DMA_UNROLL):
                fn(g * MOE_DMA_UNROLL + i, i)
            return c
        n_groups = count // MOE_DMA_UNROLL
        lax.fori_loop(0, n_groups, group, 0)

        def single(r, c):
            fn(r, 0)
            return c
        lax.fori_loop(n_groups * MOE_DMA_UNROLL, count, single, 0)

    def issue_gather(it):
        for_rows(granules(item_n_ref[it]) * MOE_RB, lambda r, i: gather_copy(it, r).start(priority=i % 2))

    def wait_tiles(count, hbm, scr, sem):
        for_rows(count, lambda r, i: pltpu.make_async_copy(tile(hbm, 0), tile(scr, 0), sem).wait())

    def weight_copies(step, slot):
        it = step // n_ft
        j = step - it * n_ft
        e = item_e_ref[it]
        col = pl.multiple_of(j * MOE_TF, MOE_TF)
        col_l = pl.multiple_of(de + j * MOE_TF, MOE_TF)
        return (pltpu.make_async_copy(w1_hbm.at[layer, e, :, pl.ds(col, MOE_TF)], w1g_f.at[slot], wsem.at[slot]),
                pltpu.make_async_copy(w1_hbm.at[layer, e, :, pl.ds(col_l, MOE_TF)], w1l_f.at[slot], wsem.at[slot]),
                pltpu.make_async_copy(w2_hbm.at[layer, e, pl.ds(col, MOE_TF), :], w2_f.at[slot], wsem.at[slot]))

    def compute(row0, m, b1g, b1l):
        rows = pl.ds(row0, m)
        x = xb_scr[rows, :]
        hg = jnp.dot(x, w1g_b[...], preferred_element_type=F32) + b1g
        hl = jnp.dot(x, w1l_b[...], preferred_element_type=F32) + b1l
        hg = jnp.minimum(hg, SWIGLU_LIMIT)
        hl = jnp.clip(hl, -SWIGLU_LIMIT, SWIGLU_LIMIT)
        act = hg * _sigmoid(SWIGLU_ALPHA * hg) * (hl + 1.0)
        acc_scr[rows, :] += jnp.dot(act.astype(BF16), w2_b[...], preferred_element_type=F32)

    for cp in weight_copies(0, 0):
        cp.start()
    issue_gather(0)

    def item_body(it, pending_rows):
        n_rows = item_n_ref[it]
        e = item_e_ref[it]
        n_g = granules(n_rows)
        wait_tiles(n_g * MOE_RB, u2p_hbm, xq_scr, gsem)

        def unpack(b, c):
            rows = pl.ds(pl.multiple_of(b * MOE_RB, MOE_RB), MOE_RB)
            for t in range(tr):
                lo, hi = _unpack_bf16_pair(xq_scr[pl.ds(b * (MOE_RB * tr) + t, MOE_RB, stride=tr), :])
                xb_scr[rows, t * LANES:(t + 1) * LANES] = lo.astype(BF16)
                xb_scr[rows, half + t * LANES:half + (t + 1) * LANES] = hi.astype(BF16)
            acc_scr[rows, :] = jnp.broadcast_to(b2_ref[pl.ds(e, 1), :], (MOE_RB, d))
            return c
        lax.fori_loop(0, n_g, unpack, 0)

        @pl.when(it + 1 < n_items)
        def _prefetch_rows():
            issue_gather(it + 1)

        def tile_step(j, c):
            step = it * n_ft + j
            slot = step & 1
            for cp in weight_copies(step, slot):
                cp.wait()

            @pl.when(step + 1 < n_steps)
            def _prefetch_weights():
                for cp in weight_copies(step + 1, 1 - slot):
                    cp.start()

            w1g_b[...] = w1g_f[slot].astype(BF16)
            w1l_b[...] = w1l_f[slot].astype(BF16)
            w2_b[...] = w2_f[slot].astype(BF16)
            b1g = b1_ref[e, pl.ds(j, 1), :]
            b1l = b1_ref[e, pl.ds(n_ft + j, 1), :]
            n_quads = n_g // 4

            def quad(q, c2):
                compute(pl.multiple_of(q * (4 * MOE_RB), 4 * MOE_RB), 4 * MOE_RB, b1g, b1l)
                return c2
            lax.fori_loop(0, n_quads, quad, 0)

            @pl.when((n_g & 2) != 0)
            def _two():
                compute(pl.multiple_of(n_quads * (4 * MOE_RB), 2 * MOE_RB), 2 * MOE_RB, b1g, b1l)

            @pl.when((n_g & 1) != 0)
            def _one():
                compute(pl.multiple_of((n_g - 1) * MOE_RB, MOE_RB), MOE_RB, b1g, b1l)
            return c
        lax.fori_loop(0, n_ft, tile_step, 0)

        wait_tiles(pending_rows, ys_scr, y4p_hbm, ssem)

        def pack(b, c):
            rows = pl.ds(pl.multiple_of(b * MOE_RB, MOE_RB), MOE_RB)
            for t in range(tr):
                lo = acc_scr[rows, t * LANES:(t + 1) * LANES]
                hi = acc_scr[rows, half + t * LANES:half + (t + 1) * LANES]
                ys_scr[pl.ds(b * (MOE_RB * tr) + t, MOE_RB, stride=tr), :] = _pack_bf16_pair(lo, hi)
            return c
        lax.fori_loop(0, n_g, pack, 0)
        for_rows(n_rows, lambda r, i: scatter_copy(it, r).start(priority=i % 2))
        return n_rows

    pending = lax.fori_loop(0, n_items, item_body, jnp.int32(0))
    wait_tiles(pending, ys_scr, y4p_hbm, ssem)


def _moe(u2p, top_i, layer, w1, b1, w2, b2):
    n = u2p.shape[0] // TOKEN_TILE_ROWS
    depth, n_experts, d, two_de = w1.shape
    de = two_de // 2
    n_ft = de // MOE_TF
    n_assign = n * TOP_K
    max_items = n_experts + n_assign // MOE_CAP
    order, item_e, item_start, item_n, n_items = _moe_items(top_i, n_experts, max_items)
    body = functools.partial(_moe_body, layer=layer, n_assign=n_assign, n_tokens=n, n_ft=n_ft)
    tiles = MOE_CAP * TOKEN_TILE_ROWS
    grid_spec = pltpu.PrefetchScalarGridSpec(
        num_scalar_prefetch=5,
        grid=(1,),
        in_specs=[
            pl.BlockSpec(memory_space=pl.ANY),
            pl.BlockSpec(memory_space=pl.ANY),
            pl.BlockSpec(memory_space=pl.ANY),
            pl.BlockSpec((None, n_experts, 2 * n_ft, MOE_TF), lambda i, *_: (layer, 0, 0, 0)),
            pl.BlockSpec((None, n_experts, d), lambda i, *_: (layer, 0, 0)),
        ],
        out_specs=pl.BlockSpec(memory_space=pl.ANY),
        scratch_shapes=[
            pltpu.VMEM((tiles, LANES), jnp.uint32), pltpu.VMEM((MOE_CAP, d), BF16),
            pltpu.VMEM((MOE_CAP, d), F32), pltpu.VMEM((tiles, LANES), jnp.uint32),
            pltpu.VMEM((2, d, MOE_TF), F32), pltpu.VMEM((2, d, MOE_TF), F32), pltpu.VMEM((2, MOE_TF, d), F32),
            pltpu.VMEM((d, MOE_TF), BF16), pltpu.VMEM((d, MOE_TF), BF16), pltpu.VMEM((MOE_TF, d), BF16),
            pltpu.SemaphoreType.DMA, pltpu.SemaphoreType.DMA, pltpu.SemaphoreType.DMA((2,)),
        ],
    )
    return pl.pallas_call(
        body,
        out_shape=jax.ShapeDtypeStruct((n_assign * TOKEN_TILE_ROWS, LANES), jnp.uint32),
        grid_spec=grid_spec,
        compiler_params=_params("arbitrary"),
        name="moe",
    )(order, item_e, item_start, item_n, n_items, u2p, w1, w2,
      b1.reshape(depth, n_experts, 2 * n_ft, MOE_TF), b2)


def _combine_body(x1_ref, y0_ref, y1_ref, y2_ref, y3_ref, tg_ref, mod_ref, modn_ref, lng_ref, lnb_ref,
                  x2_ref, un_ref, ff_scr, *, alpha):
    tm, d = x1_ref.shape
    half = d // 2
    tr = TOKEN_TILE_ROWS
    tg = tg_ref[...]
    for t in range(tr):
        lo_sum = hi_sum = None
        for k, y_ref in enumerate((y0_ref, y1_ref, y2_ref, y3_ref)):
            lo, hi = _unpack_bf16_pair(y_ref[pl.ds(t, tm, stride=tr), :])
            g = tg[:, k:k + 1]
            lo_sum = g * lo if lo_sum is None else lo_sum + g * lo
            hi_sum = g * hi if hi_sum is None else hi_sum + g * hi
        ff_scr[:, t * LANES:(t + 1) * LANES] = lo_sum
        ff_scr[:, half + t * LANES:half + (t + 1) * LANES] = hi_sum
    z = alpha * x1_ref[...] + mod_ref[5:6, :] * ff_scr[...]
    x2 = _layer_norm(z, lng_ref[...], lnb_ref[...])
    x2_ref[...] = x2
    un_ref[...] = (x2 * (1.0 + modn_ref[1:2, :]) + modn_ref[0:1, :]).astype(BF16)


def _combine(x1, y4p, tg, mod_l, mod_next, ln_g, ln_b, alpha, n_p, ds, tm=256):
    n, d = x1.shape
    body = functools.partial(_combine_body, alpha=alpha)
    row = lambda i: (i, 0)
    const = lambda i: (0, 0)
    cond = lambda i: (_cond_index(i * tm, n_p, ds), 0, 0)
    y_specs = [pl.BlockSpec((tm * TOKEN_TILE_ROWS, LANES), lambda i, k=k: (k * (n // tm) + i, 0))
               for k in range(TOP_K)]
    return pl.pallas_call(
        body,
        out_shape=(jax.ShapeDtypeStruct((n, d), F32), jax.ShapeDtypeStruct((n, d), BF16)),
        grid=(n // tm,),
        in_specs=[
            pl.BlockSpec((tm, d), row),
            *y_specs,
            pl.BlockSpec((tm, LANES), row),
            pl.BlockSpec((None, N_MOD, d), cond),
            pl.BlockSpec((None, N_MOD, d), cond),
            pl.BlockSpec((1, d), const),
            pl.BlockSpec((1, d), const),
        ],
        out_specs=(pl.BlockSpec((tm, d), row), pl.BlockSpec((tm, d), row)),
        scratch_shapes=[pltpu.VMEM((tm, d), F32)],
        compiler_params=_params("arbitrary"),
        name="combine",
    )(x1, y4p, y4p, y4p, y4p, tg, mod_l, mod_next, ln_g, ln_b)


def _block_diag_tiles(w, tc):
    depth, nb, bw, _ = w.shape
    per = tc // bw
    wt = w.reshape(depth, nb // per, per, bw, bw)
    eye = jnp.eye(per, dtype=w.dtype)
    full = jnp.einsum("lgajk,ab->lgajbk", wt, eye)
    return full.reshape(depth, nb // per, tc, tc)


def _rg_gate_weights(rg_wa, rg_ba, rg_wx, rg_bx, tc=256):
    depth = rg_wa.shape[0]
    n_ct = D_RG // tc
    tiles = [_block_diag_tiles(w[:, d], tc) for d in (0, 1) for w in (rg_wa, rg_wx)]
    wg = jnp.concatenate(tiles, axis=-1).astype(BF16)
    biases = [b[:, d].reshape(depth, n_ct, 1, tc) for d in (0, 1) for b in (rg_ba, rg_bx)]
    bg = jnp.concatenate(biases, axis=-1)
    return wg, bg


def kernel(x_prompt, x_sample, state_rglru, state_mlstm_C, state_mlstm_n, state_mlstm_m, c, c_ctx, w_ada, b_ada, w_in, ml_gate_b, rg_conv_w, rg_conv_b, rg_wa, rg_ba, rg_wx, rg_bx, rg_lambda, ml_norm_w, w_out, ln1_g, ln1_b, router_w, router_b, moe_w1, moe_b1, moe_w2, moe_b2, ln2_g, ln2_b):
    depth = w_ada.shape[0]
    bsz, seq, d = x_prompt.shape
    dbs, dseq, _ = x_sample.shape
    n_p, n_s = bsz * seq, dbs * dseq
    n_experts = router_w.shape[-1]
    alpha = float((2 * depth) ** 0.25)
    assert d == D_MODEL and seq == CHUNK and dseq % CHUNK == 0 and n_p % dseq == 0

    x = jnp.concatenate([x_prompt.reshape(n_p, d), x_sample.reshape(n_s, d)], axis=0)
    n_cond = 1 + dbs
    conds = jnp.concatenate([c_ctx[None, :], c, jnp.zeros((SUBLANES - n_cond, d), F32)], axis=0)
    mod = _ada(conds, w_ada, b_ada)
    mod = mod[:, :n_cond].reshape(depth, n_cond, N_MOD, d)

    w_in_bf = w_in.astype(BF16)
    w_g = w_in[:, :, D_PROJ:]
    wg_pad = jnp.pad(w_g, ((0, 0), (0, 0), (0, LANES - N_GATES))).astype(BF16)
    wgt = jnp.swapaxes(w_g, 1, 2).astype(BF16)
    gb_row = jnp.pad(ml_gate_b, ((0, 0), (0, LANES - N_GATES))).reshape(depth, 1, LANES)
    gb_col = ml_gate_b.reshape(depth, N_GATES, 1)
    rg_wg, rg_bg = _rg_gate_weights(rg_wa, rg_ba, rg_wx, rg_bx)
    w_out_bf = w_out.astype(BF16)
    rw_pad = jnp.pad(router_w, ((0, 0), (0, 0), (0, LANES - n_experts)))
    rb_pad = jnp.pad(router_b, ((0, 0), (0, LANES - n_experts))).reshape(depth, 1, LANES)

    zero_rg = jnp.zeros((1, 2, D_RG), F32)
    zero_c = jnp.zeros((1, 2, ML_HEADS, ML_HD, ML_HD), F32)
    zero_n = jnp.zeros((1, 2, ML_HEADS, 1, ML_HD), F32)
    zero_m = jnp.zeros((1, 2, ML_HEADS, 1, LANES), F32)

    u = _modulate(x, mod[0], n_p, dseq)
    states = []
    for l in range(depth):
        proj = _inproj(u, w_in_bf[l])
        g, gt = _gates(u, wg_pad[l], wgt[l], gb_row[l], gb_col[l])
        g_cols = g[:, :N_GATES].reshape(n_p + n_s, 4, ML_HEADS).transpose(2, 0, 1)
        g_rows = gt.reshape(4, ML_HEADS, n_p + n_s).transpose(1, 0, 2)

        conv_b = rg_conv_b[l].reshape(1, D_RG)
        rg_p, ht_p = _rglru(proj, 0, bsz, seq, seq, rg_conv_w[l], conv_b, rg_wg[l], rg_bg[l], rg_lambda[l], zero_rg)
        rg_s, _ = _rglru(proj, n_p // dseq, dbs, dseq, GRID_W, rg_conv_w[l], conv_b, rg_wg[l], rg_bg[l],
                         rg_lambda[l], state_rglru[:, l])

        norm_w = ml_norm_w[l].reshape(1, D_ML)
        ml_p, c_p, nn_p, m_p = _mlstm(proj, 0, bsz, seq, g_cols, g_rows, zero_c, zero_n, zero_m, norm_w, True)
        m0_s = jnp.broadcast_to(state_mlstm_m[:, l][..., None, None], (dbs, 2, ML_HEADS, 1, LANES))
        (ml_s,) = _mlstm(proj, n_p // dseq, dbs, dseq, g_cols, g_rows, state_mlstm_C[:, l],
                         state_mlstm_n[:, l][:, :, :, None, :], m0_s, norm_w, False)
        states.append((ht_p, c_p, nn_p[:, :, :, 0, :], m_p[:, :, :, 0, 0]))

        rg_y = jnp.concatenate([rg_p, rg_s], axis=0)
        ml_y = jnp.concatenate([ml_p, ml_s], axis=0)
        x1, u2p, ti, tg = _outproj(rg_y, ml_y, w_out_bf[l], x, mod[l], ln1_g[l].reshape(1, d), ln1_b[l].reshape(1, d),
                                  rw_pad[l], rb_pad[l], alpha, n_experts, n_p, dseq)
        y4p = _moe(u2p, ti[:, :TOP_K], l, moe_w1, moe_b1, moe_w2, moe_b2)
        x, u = _combine(x1, y4p, tg, mod[l], mod[min(l + 1, depth - 1)], ln2_g[l].reshape(1, d),
                        ln2_b[l].reshape(1, d), alpha, n_p, dseq)

    new_rglru = jnp.stack([s[0] for s in states], axis=1)
    new_c = jnp.stack([s[1] for s in states], axis=1)
    new_n = jnp.stack([s[2] for s in states], axis=1)
    new_m = jnp.stack([s[3] for s in states], axis=1)
    return (x[:n_p].reshape(bsz, seq, d), x[n_p:].reshape(dbs, dseq, d), new_rglru, new_c, new_n, new_m)
```

```python
import functools

import jax
import jax.numpy as jnp
from jax import lax
from jax.experimental import pallas as pl
from jax.experimental.pallas import tpu as pltpu

F32 = jnp.float32
BF16 = jnp.bfloat16

D_MODEL = 2048
D_RG = 1024
RG_BLOCK_W = 64
RG_CONV_W = 4
RG_C = 8.0
GRID_W = 64
D_ML = 1024
ML_HEADS = 4
ML_HD = 256
TOP_K = 4
SWIGLU_LIMIT = 7.0
SWIGLU_ALPHA = 1.702
LN_EPS = 1e-5
N_MOD = 6
CHUNK = 256
RG_SCAN_UNROLL = 4
SUBLANES = 8
LANES = 128
N_GATES = 4 * ML_HEADS
COL_Q = 2 * D_RG
COL_K = COL_Q + D_ML
COL_V = COL_K + D_ML
COL_O = COL_V + D_ML
D_PROJ = COL_O + D_ML

TOKEN_TILE_ROWS = D_MODEL // 2 // LANES
MOE_CAP = 1280
MOE_RB = 128
MOE_TF = 256
MOE_DMA_UNROLL = 16
MOE_SIDE_QUANTA = 16
MOE_QUANTUM = MOE_CAP // MOE_SIDE_QUANTA
MOE_W_SLOTS = 3
VMEM_LIMIT = 56 * 1024 * 1024
HIGHEST = lax.Precision.HIGHEST


def _pack_bf16_pair(lo, hi):
    lo_bits = lax.bitcast_convert_type(lo.astype(BF16).astype(F32), jnp.uint32)
    hi_bits = lax.bitcast_convert_type(hi.astype(BF16).astype(F32), jnp.uint32)
    return (lo_bits >> 16) | (hi_bits & jnp.uint32(0xFFFF0000))


def _unpack_bf16_pair(w):
    lo = lax.bitcast_convert_type(w << 16, F32)
    hi = lax.bitcast_convert_type(w & jnp.uint32(0xFFFF0000), F32)
    return lo, hi


def _sigmoid(x):
    return 1.0 / (1.0 + jnp.exp(-x))


def _softplus(x):
    return jnp.maximum(x, 0.0) + jnp.log1p(jnp.exp(-jnp.abs(x)))


def _gelu_tanh(x):
    return 0.5 * x * (1.0 + jnp.tanh(0.7978845608028654 * (x + 0.044715 * (x * x * x))))


def _layer_norm(z, g, b):
    mu = jnp.mean(z, axis=-1, keepdims=True)
    zc = z - mu
    var = jnp.mean(zc * zc, axis=-1, keepdims=True)
    return zc * lax.rsqrt(var + LN_EPS) * g + b


def _params(*sem):
    return pltpu.CompilerParams(dimension_semantics=sem, vmem_limit_bytes=VMEM_LIMIT)


def _cond_index(row_start, n_p, ds):
    return jnp.where(row_start < n_p, 0, 1 + (row_start - n_p) // ds)


def _ada_body(c_ref, w_ref, b_ref, o_ref):
    c = c_ref[...]
    s = c * _sigmoid(c)
    o_ref[...] = jnp.dot(s, w_ref[...], preferred_element_type=F32, precision=HIGHEST) + b_ref[...]


def _ada(conds, w_ada, b_ada, tn=1024):
    depth, d, n6 = w_ada.shape
    nc = conds.shape[0]
    return pl.pallas_call(
        _ada_body,
        out_shape=jax.ShapeDtypeStruct((depth, nc, n6), F32),
        grid=(depth, n6 // tn),
        in_specs=[
            pl.BlockSpec((nc, d), lambda l, j: (0, 0)),
            pl.BlockSpec((None, d, tn), lambda l, j: (l, 0, j)),
            pl.BlockSpec((None, 1, tn), lambda l, j: (l, 0, j)),
        ],
        out_specs=pl.BlockSpec((None, nc, tn), lambda l, j: (l, 0, j)),
        compiler_params=_params("arbitrary", "arbitrary"),
        name="ada",
    )(conds, w_ada, b_ada.reshape(depth, 1, n6))


def _modulate_body(x_ref, mod_ref, u_ref):
    u_ref[...] = (x_ref[...] * (1.0 + mod_ref[1:2, :]) + mod_ref[0:1, :]).astype(BF16)


def _modulate(x, mod_l, n_p, ds, tm=512):
    n, d = x.shape
    return pl.pallas_call(
        _modulate_body,
        out_shape=jax.ShapeDtypeStruct((n, d), BF16),
        grid=(n // tm,),
        in_specs=[
            pl.BlockSpec((tm, d), lambda i: (i, 0)),
            pl.BlockSpec((None, N_MOD, d), lambda i: (_cond_index(i * tm, n_p, ds), 0, 0)),
        ],
        out_specs=pl.BlockSpec((tm, d), lambda i: (i, 0)),
        compiler_params=_params("arbitrary"),
        name="modulate",
    )(x, mod_l)


def _inproj_body(u_ref, w_ref, o_ref):
    o_ref[...] = jnp.dot(u_ref[...], w_ref[...], preferred_element_type=F32)


def _inproj(u, w_bf, tm=1024, tn=768):
    n, d = u.shape
    return pl.pallas_call(
        _inproj_body,
        out_shape=jax.ShapeDtypeStruct((n, D_PROJ), F32),
        grid=(D_PROJ // tn, n // tm),
        in_specs=[
            pl.BlockSpec((tm, d), lambda j, i: (i, 0)),
            pl.BlockSpec((d, tn), lambda j, i: (0, j)),
        ],
        out_specs=pl.BlockSpec((tm, tn), lambda j, i: (i, j)),
        compiler_params=_params("arbitrary", "arbitrary"),
        name="inproj",
    )(u, w_bf)


def _gates_body(u_ref, wg_ref, wgt_ref, brow_ref, bcol_ref, g_ref, gt_ref):
    u = u_ref[...]
    g = jnp.dot(u, wg_ref[...], preferred_element_type=F32) + brow_ref[...]
    lane = lax.broadcasted_iota(jnp.int32, g.shape, 1)
    is_forget = ((lane >> 2) & 1) == 1
    g_ref[...] = jnp.where(is_forget, -_softplus(-g), g)
    gt = lax.dot_general(wgt_ref[...], u, (((1,), (1,)), ((), ())), preferred_element_type=F32) + bcol_ref[...]
    row = lax.broadcasted_iota(jnp.int32, gt.shape, 0)
    is_forget_t = ((row >> 2) & 1) == 1
    gt_ref[...] = jnp.where(is_forget_t, -_softplus(-gt), gt)


def _gates(u, wg_pad, wgt, brow, bcol, tm=1024):
    n, d = u.shape
    return pl.pallas_call(
        _gates_body,
        out_shape=(jax.ShapeDtypeStruct((n, LANES), F32), jax.ShapeDtypeStruct((N_GATES, n), F32)),
        grid=(n // tm,),
        in_specs=[
            pl.BlockSpec((tm, d), lambda i: (i, 0)),
            pl.BlockSpec((d, LANES), lambda i: (0, 0)),
            pl.BlockSpec((N_GATES, d), lambda i: (0, 0)),
            pl.BlockSpec((1, LANES), lambda i: (0, 0)),
            pl.BlockSpec((N_GATES, 1), lambda i: (0, 0)),
        ],
        out_specs=(pl.BlockSpec((tm, LANES), lambda i: (i, 0)), pl.BlockSpec((N_GATES, tm), lambda i: (0, i))),
        compiler_params=_params("arbitrary"),
        name="gates",
    )(u, wg_pad, wgt, brow, bcol)


def _rglru_body(rx_ref, rgate_ref, cw_ref, cb_ref, wg_ref, bg_ref, lam_ref, h0_ref, y_ref, ht_ref,
                af_scr, uf_scr, ab_scr, ub_scr, hf_scr, *, seq_len, period):
    tc = rx_ref.shape[1]
    n_chunks = seq_len // CHUNK
    n_tiles = seq_len // SUBLANES
    sp = _softplus(-lam_ref[...])

    def gate_chunk(c, carry):
        rows = pl.ds(pl.multiple_of(c * CHUNK, CHUNK), CHUNK)
        x = rx_ref[rows, :]
        t = lax.broadcasted_iota(jnp.int32, (CHUNK, 1), 0) & (period - 1)
        xm2 = jnp.where(t >= 2, pltpu.roll(x, 2, 0), 0.0)
        xm1 = jnp.where(t >= 1, pltpu.roll(x, 1, 0), 0.0)
        xp1 = jnp.where(t <= period - 2, pltpu.roll(x, CHUNK - 1, 0), 0.0)
        xc = (cw_ref[0:1, :] * xm2 + cw_ref[1:2, :] * xm1 + cw_ref[2:3, :] * x + cw_ref[3:4, :] * xp1
              + cb_ref[...])
        g = jnp.dot(xc.astype(BF16), wg_ref[...], preferred_element_type=F32) + bg_ref[...]
        for d, (a_scr, u_scr) in enumerate(((af_scr, uf_scr), (ab_scr, ub_scr))):
            r = _sigmoid(g[:, (2 * d) * tc:(2 * d + 1) * tc])
            i = _sigmoid(g[:, (2 * d + 1) * tc:(2 * d + 2) * tc])
            log_a = (-RG_C) * r * sp[d:d + 1, :]
            a = jnp.exp(log_a)
            a_scr[rows, :] = a
            u_scr[rows, :] = jnp.sqrt(-jnp.tanh(log_a) * (a * a + 1.0)) * (i * xc)
        return carry

    lax.fori_loop(0, n_chunks, gate_chunk, 0)

    sub = lax.broadcasted_iota(jnp.int32, (SUBLANES, 1), 0)

    def tile_scan(a, u, reverse):
        for d in (1, 2, 4):
            if reverse:
                valid = sub < SUBLANES - d
                shift = SUBLANES - d
            else:
                valid = sub >= d
                shift = d
            a_sh = pltpu.roll(a, shift, 0)
            u_sh = pltpu.roll(u, shift, 0)
            u = jnp.where(valid, a * u_sh + u, u)
            a = jnp.where(valid, a * a_sh, a)
        return a, u

    def fwd(i, h_prev):
        rows = pl.ds(pl.multiple_of(i * SUBLANES, SUBLANES), SUBLANES)
        a, u = tile_scan(af_scr[rows, :], uf_scr[rows, :], False)
        h = a * h_prev + u
        hf_scr[rows, :] = h
        return jnp.broadcast_to(h[SUBLANES - 1:SUBLANES, :], h.shape)

    h0f = jnp.broadcast_to(h0_ref[0:1, :], (SUBLANES, tc))
    h_last = lax.fori_loop(0, n_tiles, fwd, h0f, unroll=RG_SCAN_UNROLL)
    ht_ref[0:1, :] = h_last[0:1, :]

    def bwd(k, h_next):
        i = n_tiles - 1 - k
        rows = pl.ds(pl.multiple_of(i * SUBLANES, SUBLANES), SUBLANES)
        a, u = tile_scan(ab_scr[rows, :], ub_scr[rows, :], True)
        h = a * h_next + u
        y = (hf_scr[rows, :] + h) * _gelu_tanh(rgate_ref[rows, :])
        y_ref[rows, :] = y.astype(y_ref.dtype)
        return jnp.broadcast_to(h[0:1, :], h.shape)

    h0b = jnp.broadcast_to(h0_ref[1:2, :], (SUBLANES, tc))
    h_first = lax.fori_loop(0, n_tiles, bwd, h0b, unroll=RG_SCAN_UNROLL)
    ht_ref[1:2, :] = h_first[0:1, :]


def _rglru(proj, row_block0, n_seq, seq_len, period, conv_w, conv_b, wg, bg, lam, h0, tc=256):
    n_ct = D_RG // tc
    body = functools.partial(_rglru_body, seq_len=seq_len, period=period)
    h0_map = (lambda s, g: (s, 0, g)) if h0.shape[0] > 1 else (lambda s, g: (0, 0, g))
    return pl.pallas_call(
        body,
        out_shape=(jax.ShapeDtypeStruct((n_seq * seq_len, D_RG), BF16),
                   jax.ShapeDtypeStruct((n_seq, 2, D_RG), F32)),
        grid=(n_seq, n_ct),
        in_specs=[
            pl.BlockSpec((seq_len, tc), lambda s, g: (row_block0 + s, g)),
            pl.BlockSpec((seq_len, tc), lambda s, g: (row_block0 + s, n_ct + g)),
            pl.BlockSpec((RG_CONV_W, tc), lambda s, g: (0, g)),
            pl.BlockSpec((1, tc), lambda s, g: (0, g)),
            pl.BlockSpec((None, tc, 4 * tc), lambda s, g: (g, 0, 0)),
            pl.BlockSpec((None, 1, 4 * tc), lambda s, g: (g, 0, 0)),
            pl.BlockSpec((2, tc), lambda s, g: (0, g)),
            pl.BlockSpec((None, 2, tc), h0_map),
        ],
        out_specs=(pl.BlockSpec((seq_len, tc), lambda s, g: (s, g)),
                   pl.BlockSpec((None, 2, tc), lambda s, g: (s, 0, g))),
        scratch_shapes=[pltpu.VMEM((seq_len, tc), F32) for _ in range(5)],
        compiler_params=_params("arbitrary", "arbitrary"),
        name="rglru",
    )(proj, proj, conv_w, conv_b, wg, bg, lam, h0)


def _mlstm_body(q_ref, k_ref, v_ref, og_ref, gc_ref, gr_ref, c0_ref, n0_ref, m0_ref, nw_ref,
                y_ref, *rest, seq_len, with_state):
    if with_state:
        c_out, n_out, m_out, hf_scr, c_scr, n_scr, m_scr = rest
    else:
        hf_scr, c_scr, n_scr, m_scr = rest
    L = CHUNK
    n_chunks = seq_len // L
    ii = lax.broadcasted_iota(jnp.int32, (L, L), 0)
    jj = lax.broadcasted_iota(jnp.int32, (L, L), 1)
    scale = ML_HD ** -0.5

    def chunk_step(rows, direction, update_state):
        mask = (jj <= ii) if direction == 0 else (jj >= ii)
        mask_t = (ii <= jj) if direction == 0 else (ii >= jj)
        ig_r = gr_ref[2 * direction:2 * direction + 1, rows]
        lf_r = gr_ref[2 * direction + 1:2 * direction + 2, rows]
        ig_c = gc_ref[rows, 2 * direction:2 * direction + 1]
        lf_c = gc_ref[rows, 2 * direction + 1:2 * direction + 2]
        b_col = jnp.sum(jnp.where(mask, lf_r, 0.0), axis=1, keepdims=True)
        b_row = jnp.sum(jnp.where(mask_t, lf_c, 0.0), axis=0, keepdims=True)
        total = jnp.sum(lf_r, axis=1, keepdims=True)
        m_prev = m_scr[0:1, 0:1]
        dmat = jnp.where(mask, b_col - b_row + ig_r, -jnp.inf)
        inter = b_col + m_prev
        m_t = jnp.maximum(inter, jnp.max(dmat, axis=1, keepdims=True))
        w_inter = jnp.exp(inter - m_t)
        q = q_ref[rows, :]
        k = k_ref[rows, :] * scale
        v = v_ref[rows, :]
        qb, kb, vb = q.astype(BF16), k.astype(BF16), v.astype(BF16)
        qk = lax.dot_general(qb, kb, (((1,), (1,)), ((), ())), preferred_element_type=F32)
        s = qk * jnp.exp(dmat - m_t)
        c_prev = c_scr[...]
        n_prev = n_scr[...]
        num = (w_inter * jnp.dot(qb, c_prev.astype(BF16), preferred_element_type=F32)
               + jnp.dot(s.astype(BF16), vb, preferred_element_type=F32))
        den = w_inter * jnp.sum(q * n_prev, axis=1, keepdims=True) + jnp.sum(s, axis=1, keepdims=True)
        h = num / jnp.maximum(jnp.abs(den), jnp.exp(-m_t))
        if update_state:
            g_row = total - b_row + ig_r
            g_col = total - b_col + ig_c
            m_new = jnp.maximum(total + m_prev, jnp.max(g_row, axis=1, keepdims=True))
            decay = jnp.exp(total + m_prev - m_new)
            kw = k * jnp.exp(g_col - m_new)
            c_scr[...] = decay * c_prev + lax.dot_general(
                kw.astype(BF16), vb, (((0,), (0,)), ((), ())), preferred_element_type=F32)
            n_scr[...] = decay * n_prev + jnp.sum(kw, axis=0, keepdims=True)
            m_scr[...] = jnp.broadcast_to(m_new, m_scr.shape)
        return h

    for direction in (0, 1):
        c_scr[...] = c0_ref[direction]
        n_scr[...] = n0_ref[direction]
        m_scr[...] = m0_ref[direction]

        def body(step, carry, direction=direction):
            c = step if direction == 0 else n_chunks - 1 - step
            rows = pl.ds(pl.multiple_of(c * L, L), L)
            h = chunk_step(rows, direction, with_state or n_chunks > 1)
            if direction == 0:
                hf_scr[rows, :] = h
            else:
                hm = hf_scr[rows, :] + h
                mu = jnp.mean(hm, axis=1, keepdims=True)
                hc = hm - mu
                var = jnp.mean(hc * hc, axis=1, keepdims=True)
                hn = hc * lax.rsqrt(var + LN_EPS) * nw_ref[...]
                y_ref[rows, :] = (_sigmoid(og_ref[rows, :]) * hn).astype(y_ref.dtype)
            return carry

        lax.fori_loop(0, n_chunks, body, 0)
        if with_state:
            c_out[direction] = c_scr[...]
            n_out[direction] = n_scr[...]
            m_out[direction] = m_scr[...]


def _mlstm(proj, row_block0, n_seq, seq_len, g_cols, g_rows, c0, n0, m0, norm_w, with_state):
    hd = ML_HD
    body = functools.partial(_mlstm_body, seq_len=seq_len, with_state=with_state)
    bcast = c0.shape[0] == 1
    smap = (lambda s: 0) if bcast else (lambda s: s)
    col = lambda base: (lambda s, h: (row_block0 + s, base // hd + h))
    out_shape = [jax.ShapeDtypeStruct((n_seq * seq_len, D_ML), BF16)]
    out_specs = [pl.BlockSpec((seq_len, hd), lambda s, h: (s, h))]
    if with_state:
        out_shape += [jax.ShapeDtypeStruct((n_seq, 2, ML_HEADS, hd, hd), F32),
                      jax.ShapeDtypeStruct((n_seq, 2, ML_HEADS, 1, hd), F32),
                      jax.ShapeDtypeStruct((n_seq, 2, ML_HEADS, 1, LANES), F32)]
        out_specs += [pl.BlockSpec((None, 2, None, hd, hd), lambda s, h: (s, 0, h, 0, 0)),
                      pl.BlockSpec((None, 2, None, 1, hd), lambda s, h: (s, 0, h, 0, 0)),
                      pl.BlockSpec((None, 2, None, 1, LANES), lambda s, h: (s, 0, h, 0, 0))]
    return pl.pallas_call(
        body,
        out_shape=tuple(out_shape),
        grid=(n_seq, ML_HEADS),
        in_specs=[
            pl.BlockSpec((seq_len, hd), col(COL_Q)),
            pl.BlockSpec((seq_len, hd), col(COL_K)),
            pl.BlockSpec((seq_len, hd), col(COL_V)),
            pl.BlockSpec((seq_len, hd), col(COL_O)),
            pl.BlockSpec((None, seq_len, 4), lambda s, h: (h, row_block0 + s, 0)),
            pl.BlockSpec((None, 4, seq_len), lambda s, h: (h, 0, row_block0 + s)),
            pl.BlockSpec((None, 2, None, hd, hd), lambda s, h: (smap(s), 0, h, 0, 0)),
            pl.BlockSpec((None, 2, None, 1, hd), lambda s, h: (smap(s), 0, h, 0, 0)),
            pl.BlockSpec((None, 2, None, 1, LANES), lambda s, h: (smap(s), 0, h, 0, 0)),
            pl.BlockSpec((1, hd), lambda s, h: (0, h)),
        ],
        out_specs=tuple(out_specs),
        scratch_shapes=[pltpu.VMEM((seq_len, hd), F32), pltpu.VMEM((hd, hd), F32),
                        pltpu.VMEM((1, hd), F32), pltpu.VMEM((1, LANES), F32)],
        compiler_params=_params("arbitrary", "arbitrary"),
        name="mlstm",
    )(proj, proj, proj, proj, g_cols, g_rows, c0, n0, m0, norm_w)


def _outproj_body(rg_ref, ml_ref, w_ref, x_ref, mod_ref, lng_ref, lnb_ref, rw_ref, rb_ref,
                  x1_ref, u2p_ref, ti_ref, tg_ref, *, alpha, n_experts):
    mix = (jnp.dot(rg_ref[...], w_ref[0:D_RG, :], preferred_element_type=F32)
           + jnp.dot(ml_ref[...], w_ref[D_RG:D_MODEL, :], preferred_element_type=F32))
    z = alpha * x_ref[...] + mod_ref[2:3, :] * mix
    x1 = _layer_norm(z, lng_ref[...], lnb_ref[...])
    x1_ref[...] = x1
    u2 = x1 * (1.0 + mod_ref[4:5, :]) + mod_ref[3:4, :]
    tm = u2.shape[0]
    for s in range(TOKEN_TILE_ROWS):
        lo = u2[:, s * LANES:(s + 1) * LANES]
        hi = u2[:, D_MODEL // 2 + s * LANES:D_MODEL // 2 + (s + 1) * LANES]
        u2p_ref[pl.ds(s, tm, stride=TOKEN_TILE_ROWS), :] = _pack_bf16_pair(lo, hi)
    logits = jnp.dot(u2, rw_ref[...], preferred_element_type=F32, precision=HIGHEST) + rb_ref[...]
    lane = lax.broadcasted_iota(jnp.int32, logits.shape, 1)
    logits = jnp.where(lane < n_experts, logits, -jnp.inf)
    vals, idxs = [], []
    for _ in range(TOP_K):
        m = jnp.max(logits, axis=1, keepdims=True)
        idx = jnp.min(jnp.where(logits == m, lane, LANES), axis=1, keepdims=True)
        vals.append(m)
        idxs.append(idx)
        logits = jnp.where(lane == idx, -jnp.inf, logits)
    exps = [jnp.exp(v - vals[0]) for v in vals]
    inv = 1.0 / (exps[0] + exps[1] + exps[2] + exps[3])
    ti = jnp.zeros(lane.shape, jnp.int32)
    tg = jnp.zeros(lane.shape, F32)
    for k in range(TOP_K):
        ti = jnp.where(lane == k, idxs[k], ti)
        tg = jnp.where(lane == k, exps[k] * inv, tg)
    ti_ref[...] = ti
    tg_ref[...] = tg


def _outproj(rg_y, ml_y, w_out_bf, x, mod_l, ln_g, ln_b, rw_pad, rb_pad, alpha, n_experts, n_p, ds, tm=512):
    n, d = x.shape
    body = functools.partial(_outproj_body, alpha=alpha, n_experts=n_experts)
    row = lambda i: (i, 0)
    const = lambda i: (0, 0)
    return pl.pallas_call(
        body,
        out_shape=(jax.ShapeDtypeStruct((n, d), F32),
                   jax.ShapeDtypeStruct((n * TOKEN_TILE_ROWS, LANES), jnp.uint32),
                   jax.ShapeDtypeStruct((n, LANES), jnp.int32), jax.ShapeDtypeStruct((n, LANES), F32)),
        grid=(n // tm,),
        in_specs=[
            pl.BlockSpec((tm, D_RG), row),
            pl.BlockSpec((tm, D_ML), row),
            pl.BlockSpec((d, d), const),
            pl.BlockSpec((tm, d), row),
            pl.BlockSpec((None, N_MOD, d), lambda i: (_cond_index(i * tm, n_p, ds), 0, 0)),
            pl.BlockSpec((1, d), const),
            pl.BlockSpec((1, d), const),
            pl.BlockSpec((d, LANES), const),
            pl.BlockSpec((1, LANES), const),
        ],
        out_specs=(pl.BlockSpec((tm, d), row), pl.BlockSpec((tm * TOKEN_TILE_ROWS, LANES), row),
                   pl.BlockSpec((tm, LANES), row), pl.BlockSpec((tm, LANES), row)),
        compiler_params=_params("arbitrary"),
        name="outproj",
    )(rg_y, ml_y, w_out_bf, x, mod_l, ln_g, ln_b, rw_pad, rb_pad)


def _moe_items(top_i, n_experts, max_items):
    n_assign = top_i.size
    e_flat = top_i.reshape(n_assign)
    order = jnp.argsort(e_flat, stable=True).astype(jnp.int32)
    counts = jnp.sum((e_flat[:, None] == jnp.arange(n_experts, dtype=jnp.int32)[None, :]).astype(jnp.int32), axis=0)
    starts = jnp.cumsum(counts) - counts
    items_per_e = (counts + MOE_CAP - 1) // MOE_CAP
    item_ends = jnp.cumsum(items_per_e)
    total = item_ends[-1]
    slot = jnp.arange(max_items, dtype=jnp.int32)
    valid = slot < total
    slot_c = jnp.minimum(slot, total - 1)
    e_s = jnp.minimum(jnp.sum((slot_c[:, None] >= item_ends[None, :]).astype(jnp.int32), axis=1), n_experts - 1)
    local = slot_c - (item_ends[e_s] - items_per_e[e_s])
    start = starts[e_s] + local * MOE_CAP
    n_rows = jnp.clip(counts[e_s] - local * MOE_CAP, 0, MOE_CAP)
    return (order, e_s, jnp.where(valid, start, 0).astype(jnp.int32),
            jnp.where(valid, n_rows, 0).astype(jnp.int32), total.reshape(1).astype(jnp.int32))


def _moe_body(src_row_ref, dst_row_ref, item_e_ref, item_start_ref, item_n_ref, n_items_ref,
              u2p_hbm, w1_hbm, w2_hbm, b1_ref, b2_ref,
              y4p_hbm,
              xq_scr, xb_scr, acc_scr, ys_scr, w1g_f, w1l_f, w2_f, w1g_b, w1l_b, w2_b, gsem, ssem, wsem,
              *, layer, n_assign, n_tokens, n_ft):
    d = acc_scr.shape[1]
    half = d // 2
    de = n_ft * MOE_TF
    tr = TOKEN_TILE_ROWS
    n_items = n_items_ref[0]
    n_steps = n_items * n_ft

    def tile(ref, idx):
        return ref.at[pl.ds(pl.multiple_of(idx * tr, tr), tr)]

    def rows8(ref, row):
        return ref.at[pl.ds(pl.multiple_of(row, tr), tr)]

    def granules(n_rows):
        return (n_rows + MOE_RB - 1) // MOE_RB

    def gather_quantum(it, k, lo=0, hi=MOE_QUANTUM):
        base = item_start_ref[it] + k * MOE_QUANTUM
        for i in range(lo, hi):
            src = src_row_ref[jnp.minimum(base + i, n_assign - 1)]
            pltpu.make_async_copy(rows8(u2p_hbm, src), tile(xq_scr, k * MOE_QUANTUM + i),
                                  gsem).start(priority=i % 2)

    def scatter_quantum(it, n_valid, k, lo=0, hi=MOE_QUANTUM):
        base = item_start_ref[it] + k * MOE_QUANTUM
        for i in range(lo, hi):
            r = k * MOE_QUANTUM + i
            real = dst_row_ref[jnp.minimum(base + i, n_assign - 1)]
            dst = jnp.where(r < n_valid, real, (TOP_K * n_tokens + r) * tr)
            pltpu.make_async_copy(tile(ys_scr, r), rows8(y4p_hbm, dst), ssem).start(priority=i % 2)

    def quanta(lo, hi, fn):
        def f(k, c):
            fn(k)
            return c
        lax.fori_loop(lo, hi, f, 0)

    def wait_tiles(count, src, dst, sem):
        def group(g, c):
            for _ in range(MOE_DMA_UNROLL):
                pltpu.make_async_copy(tile(src, 0), tile(dst, 0), sem).wait()
            return c
        lax.fori_loop(0, count // MOE_DMA_UNROLL, group, 0)

    def weight_copies(step):
        it = step // n_ft
        j = step - it * n_ft
        e = item_e_ref[it]
        slot = lax.rem(step, MOE_W_SLOTS)
        col = pl.multiple_of(j * MOE_TF, MOE_TF)
        col_l = pl.multiple_of(de + j * MOE_TF, MOE_TF)
        return (pltpu.make_async_copy(w1_hbm.at[layer, e, :, pl.ds(col, MOE_TF)], w1g_f.at[slot], wsem.at[slot]),
                pltpu.make_async_copy(w1_hbm.at[layer, e, :, pl.ds(col_l, MOE_TF)], w1l_f.at[slot], wsem.at[slot]),
                pltpu.make_async_copy(w2_hbm.at[layer, e, pl.ds(col, MOE_TF), :], w2_f.at[slot], wsem.at[slot]))

    def cast_part(part, src, dst):
        h = MOE_TF // 2
        if part == 0:
            w1g_b[dst] = w1g_f[src].astype(BF16)
            w2_b[dst, 0:h, :] = w2_f[src, 0:h, :].astype(BF16)
        else:
            w1l_b[dst] = w1l_f[src].astype(BF16)
            w2_b[dst, h:MOE_TF, :] = w2_f[src, h:MOE_TF, :].astype(BF16)

    def compute(row0, m, wb, b1g, b1l, after_dot1=None, after_dot2=None):
        rows = pl.ds(row0, m)
        x = xb_scr[rows, :]
        hg = jnp.dot(x, w1g_b[wb], preferred_element_type=F32) + b1g
        if after_dot1 is not None:
            after_dot1()
        hl = jnp.dot(x, w1l_b[wb], preferred_element_type=F32) + b1l
        if after_dot2 is not None:
            after_dot2()
        hg = jnp.minimum(hg, SWIGLU_LIMIT)
        hl = jnp.clip(hl, -SWIGLU_LIMIT, SWIGLU_LIMIT)
        act = hg * _sigmoid(SWIGLU_ALPHA * hg) * (hl + 1.0)
        acc_scr[rows, :] += jnp.dot(act.astype(BF16), w2_b[wb], preferred_element_type=F32)

    for cp in weight_copies(0):
        cp.start()
    for cp in weight_copies(1):
        cp.start()

    def zero_ys(b, c):
        ys_scr[pl.ds(pl.multiple_of(b * (MOE_RB * tr), MOE_RB * tr), MOE_RB * tr), :] = jnp.zeros(
            (MOE_RB * tr, LANES), jnp.uint32)
        return c
    lax.fori_loop(0, MOE_CAP // MOE_RB, zero_ys, 0)
    spare = pltpu.make_async_copy(ys_scr, y4p_hbm.at[pl.ds(TOP_K * n_tokens * tr, MOE_CAP * tr)], ssem)
    spare.start()
    spare.wait()
    for cp in weight_copies(0):
        cp.wait()
    cast_part(0, 0, 0)
    cast_part(1, 0, 0)

    def item_body(it, carry):
        n_prev, g_issued = carry
        it_prev = jnp.maximum(it - 1, 0)
        it_next = jnp.minimum(it + 1, n_items - 1)
        n_rows = item_n_ref[it]
        e = item_e_ref[it]
        n_g = granules(n_rows)
        n_quads = n_g // 4
        q_per = jnp.minimum(n_quads, MOE_SIDE_QUANTA // n_ft)
        quanta(g_issued, MOE_SIDE_QUANTA, lambda k: gather_quantum(it, k))
        wait_tiles(MOE_CAP, u2p_hbm, xq_scr, gsem)

        def unpack(b, c):
            rows = pl.ds(pl.multiple_of(b * MOE_RB, MOE_RB), MOE_RB)
            for t in range(tr):
                lo, hi = _unpack_bf16_pair(xq_scr[pl.ds(b * (MOE_RB * tr) + t, MOE_RB, stride=tr), :])
                xb_scr[rows, t * LANES:(t + 1) * LANES] = lo.astype(BF16)
                xb_scr[rows, half + t * LANES:half + (t + 1) * LANES] = hi.astype(BF16)
            acc_scr[rows, :] = jnp.broadcast_to(b2_ref[pl.ds(e, 1), :], (MOE_RB, d))
            return c
        lax.fori_loop(0, n_g, unpack, 0)

        def tile_step(j, c):
            step = it * n_ft + j
            wb = step & 1
            nxt = lax.rem(step + 1, MOE_W_SLOTS)

            @pl.when(step + 2 < n_steps)
            def _prefetch_weights():
                for cp in weight_copies(step + 2):
                    cp.start()

            @pl.when(step + 1 < n_steps)
            def _next_weights_ready():
                for cp in weight_copies(step + 1):
                    cp.wait()

            b1g = b1_ref[e, pl.ds(j, 1), :]
            b1l = b1_ref[e, pl.ds(n_ft + j, 1), :]
            for part in range(MOE_SIDE_QUANTA // n_ft):
                @pl.when(n_quads > part)
                def _quad(part=part):
                    k = j * q_per + part
                    h = MOE_QUANTUM // 2
                    scatter_quantum(it_prev, n_prev, k, 0, h)
                    compute(part * (4 * MOE_RB), 4 * MOE_RB, wb, b1g, b1l,
                            after_dot1=lambda: gather_quantum(it_next, k, 0, h),
                            after_dot2=lambda: scatter_quantum(it_prev, n_prev, k, h, MOE_QUANTUM))
                    cast_part(part, nxt, 1 - wb)
                    gather_quantum(it_next, k, h, MOE_QUANTUM)

                @pl.when(n_quads <= part)
                def _no_quad(part=part):
                    cast_part(part, nxt, 1 - wb)

            @pl.when((n_g & 2) != 0)
            def _two():
                compute(pl.multiple_of(n_quads * (4 * MOE_RB), 2 * MOE_RB), 2 * MOE_RB, wb, b1g, b1l)

            @pl.when((n_g & 1) != 0)
            def _one():
                compute(pl.multiple_of((n_g - 1) * MOE_RB, MOE_RB), MOE_RB, wb, b1g, b1l)
            return c
        lax.fori_loop(0, n_ft, tile_step, 0)

        issued = n_ft * q_per
        quanta(issued, MOE_SIDE_QUANTA, lambda k: scatter_quantum(it_prev, n_prev, k))
        wait_tiles(MOE_CAP, ys_scr, y4p_hbm, ssem)

        def pack(b, c):
            rows = pl.ds(pl.multiple_of(b * MOE_RB, MOE_RB), MOE_RB)
            for t in range(tr):
                lo = acc_scr[rows, t * LANES:(t + 1) * LANES]
                hi = acc_scr[rows, half + t * LANES:half + (t + 1) * LANES]
                ys_scr[pl.ds(b * (MOE_RB * tr) + t, MOE_RB, stride=tr), :] = _pack_bf16_pair(lo, hi)
            return c
        lax.fori_loop(0, n_g, pack, 0)
        return n_rows, issued

    n_last, g_extra = lax.fori_loop(0, n_items, item_body, (jnp.int32(0), jnp.int32(0)))
    wait_tiles(g_extra * MOE_QUANTUM, u2p_hbm, xq_scr, gsem)
    quanta(0, MOE_SIDE_QUANTA, lambda k: scatter_quantum(n_items - 1, n_last, k))
    wait_tiles(MOE_CAP, ys_scr, y4p_hbm, ssem)


def _moe(u2p, top_i, layer, w1, b1, w2, b2):
    n = u2p.shape[0] // TOKEN_TILE_ROWS
    depth, n_experts, d, two_de = w1.shape
    de = two_de // 2
    n_ft = de // MOE_TF
    n_assign = n * TOP_K
    max_items = n_experts + n_assign // MOE_CAP
    order, item_e, item_start, item_n, n_items = _moe_items(top_i, n_experts, max_items)
    src_row = (order // TOP_K) * TOKEN_TILE_ROWS
    dst_row = ((order % TOP_K) * n + order // TOP_K) * TOKEN_TILE_ROWS
    assert MOE_SIDE_QUANTA % n_ft == 0 and MOE_SIDE_QUANTA // n_ft >= MOE_CAP // (4 * MOE_RB)
    body = functools.partial(_moe_body, layer=layer, n_assign=n_assign, n_tokens=n, n_ft=n_ft)
    tiles = MOE_CAP * TOKEN_TILE_ROWS
    grid_spec = pltpu.PrefetchScalarGridSpec(
        num_scalar_prefetch=6,
        grid=(1,),
        in_specs=[
            pl.BlockSpec(memory_space=pl.ANY),
            pl.BlockSpec(memory_space=pl.ANY),
            pl.BlockSpec(memory_space=pl.ANY),
            pl.BlockSpec((None, n_experts, 2 * n_ft, MOE_TF), lambda i, *_: (layer, 0, 0, 0)),
            pl.BlockSpec((None, n_experts, d), lambda i, *_: (layer, 0, 0)),
        ],
        out_specs=pl.BlockSpec(memory_space=pl.ANY),
        scratch_shapes=[
            pltpu.VMEM((tiles, LANES), jnp.uint32), pltpu.VMEM((MOE_CAP, d), BF16),
            pltpu.VMEM((MOE_CAP, d), F32), pltpu.VMEM((tiles, LANES), jnp.uint32),
            pltpu.VMEM((MOE_W_SLOTS, d, MOE_TF), F32), pltpu.VMEM((MOE_W_SLOTS, d, MOE_TF), F32),
            pltpu.VMEM((MOE_W_SLOTS, MOE_TF, d), F32),
            pltpu.VMEM((2, d, MOE_TF), BF16), pltpu.VMEM((2, d, MOE_TF), BF16), pltpu.VMEM((2, MOE_TF, d), BF16),
            pltpu.SemaphoreType.DMA, pltpu.SemaphoreType.DMA, pltpu.SemaphoreType.DMA((MOE_W_SLOTS,)),
        ],
    )
    return pl.pallas_call(
        body,
        out_shape=jax.ShapeDtypeStruct(((n_assign + MOE_CAP) * TOKEN_TILE_ROWS, LANES), jnp.uint32),
        grid_spec=grid_spec,
        compiler_params=_params("arbitrary"),
        name="moe",
    )(src_row, dst_row, item_e, item_start, item_n, n_items, u2p, w1, w2,
      b1.reshape(depth, n_experts, 2 * n_ft, MOE_TF), b2)


def _combine_body(x1_ref, y0_ref, y1_ref, y2_ref, y3_ref, tg_ref, mod_ref, modn_ref, lng_ref, lnb_ref,
                  x2_ref, un_ref, ff_scr, *, alpha):
    tm, d = x1_ref.shape
    half = d // 2
    tr = TOKEN_TILE_ROWS
    tg = tg_ref[...]
    for t in range(tr):
        lo_sum = hi_sum = None
        for k, y_ref in enumerate((y0_ref, y1_ref, y2_ref, y3_ref)):
            lo, hi = _unpack_bf16_pair(y_ref[pl.ds(t, tm, stride=tr), :])
            g = tg[:, k:k + 1]
            lo_sum = g * lo if lo_sum is None else lo_sum + g * lo
            hi_sum = g * hi if hi_sum is None else hi_sum + g * hi
        ff_scr[:, t * LANES:(t + 1) * LANES] = lo_sum
        ff_scr[:, half + t * LANES:half + (t + 1) * LANES] = hi_sum
    z = alpha * x1_ref[...] + mod_ref[5:6, :] * ff_scr[...]
    x2 = _layer_norm(z, lng_ref[...], lnb_ref[...])
    x2_ref[...] = x2
    un_ref[...] = (x2 * (1.0 + modn_ref[1:2, :]) + modn_ref[0:1, :]).astype(BF16)


def _combine(x1, y4p, tg, mod_l, mod_next, ln_g, ln_b, alpha, n_p, ds, tm=256):
    n, d = x1.shape
    body = functools.partial(_combine_body, alpha=alpha)
    row = lambda i: (i, 0)
    const = lambda i: (0, 0)
    cond = lambda i: (_cond_index(i * tm, n_p, ds), 0, 0)
    y_specs = [pl.BlockSpec((tm * TOKEN_TILE_ROWS, LANES), lambda i, k=k: (k * (n // tm) + i, 0))
               for k in range(TOP_K)]
    return pl.pallas_call(
        body,
        out_shape=(jax.ShapeDtypeStruct((n, d), F32), jax.ShapeDtypeStruct((n, d), BF16)),
        grid=(n // tm,),
        in_specs=[
            pl.BlockSpec((tm, d), row),
            *y_specs,
            pl.BlockSpec((tm, LANES), row),
            pl.BlockSpec((None, N_MOD, d), cond),
            pl.BlockSpec((None, N_MOD, d), cond),
            pl.BlockSpec((1, d), const),
            pl.BlockSpec((1, d), const),
        ],
        out_specs=(pl.BlockSpec((tm, d), row), pl.BlockSpec((tm, d), row)),
        scratch_shapes=[pltpu.VMEM((tm, d), F32)],
        compiler_params=_params("arbitrary"),
        name="combine",
    )(x1, y4p, y4p, y4p, y4p, tg, mod_l, mod_next, ln_g, ln_b)


def _block_diag_tiles(w, tc):
    depth, nb, bw, _ = w.shape
    per = tc // bw
    wt = w.reshape(depth, nb // per, per, bw, bw)
    eye = jnp.eye(per, dtype=w.dtype)
    full = jnp.einsum("lgajk,ab->lgajbk", wt, eye)
    return full.reshape(depth, nb // per, tc, tc)


def _rg_gate_weights(rg_wa, rg_ba, rg_wx, rg_bx, tc=256):
    depth = rg_wa.shape[0]
    n_ct = D_RG // tc
    tiles = [_block_diag_tiles(w[:, d], tc) for d in (0, 1) for w in (rg_wa, rg_wx)]
    wg = jnp.concatenate(tiles, axis=-1).astype(BF16)
    biases = [b[:, d].reshape(depth, n_ct, 1, tc) for d in (0, 1) for b in (rg_ba, rg_bx)]
    bg = jnp.concatenate(biases, axis=-1)
    return wg, bg


def kernel(x_prompt, x_sample, state_rglru, state_mlstm_C, state_mlstm_n, state_mlstm_m, c, c_ctx, w_ada, b_ada, w_in, ml_gate_b, rg_conv_w, rg_conv_b, rg_wa, rg_ba, rg_wx, rg_bx, rg_lambda, ml_norm_w, w_out, ln1_g, ln1_b, router_w, router_b, moe_w1, moe_b1, moe_w2, moe_b2, ln2_g, ln2_b):
    depth = w_ada.shape[0]
    bsz, seq, d = x_prompt.shape
    dbs, dseq, _ = x_sample.shape
    n_p, n_s = bsz * seq, dbs * dseq
    n_experts = router_w.shape[-1]
    alpha = float((2 * depth) ** 0.25)
    assert d == D_MODEL and seq == CHUNK and dseq % CHUNK == 0 and n_p % dseq == 0

    x = jnp.concatenate([x_prompt.reshape(n_p, d), x_sample.reshape(n_s, d)], axis=0)
    n_cond = 1 + dbs
    conds = jnp.concatenate([c_ctx[None, :], c, jnp.zeros((SUBLANES - n_cond, d), F32)], axis=0)
    mod = _ada(conds, w_ada, b_ada)
    mod = mod[:, :n_cond].reshape(depth, n_cond, N_MOD, d)

    w_in_bf = w_in.astype(BF16)
    w_g = w_in[:, :, D_PROJ:]
    wg_pad = jnp.pad(w_g, ((0, 0), (0, 0), (0, LANES - N_GATES))).astype(BF16)
    wgt = jnp.swapaxes(w_g, 1, 2).astype(BF16)
    gb_row = jnp.pad(ml_gate_b, ((0, 0), (0, LANES - N_GATES))).reshape(depth, 1, LANES)
    gb_col = ml_gate_b.reshape(depth, N_GATES, 1)
    rg_wg, rg_bg = _rg_gate_weights(rg_wa, rg_ba, rg_wx, rg_bx)
    w_out_bf = w_out.astype(BF16)
    rw_pad = jnp.pad(router_w, ((0, 0), (0, 0), (0, LANES - n_experts)))
    rb_pad = jnp.pad(router_b, ((0, 0), (0, LANES - n_experts))).reshape(depth, 1, LANES)

    zero_rg = jnp.zeros((1, 2, D_RG), F32)
    zero_c = jnp.zeros((1, 2, ML_HEADS, ML_HD, ML_HD), F32)
    zero_n = jnp.zeros((1, 2, ML_HEADS, 1, ML_HD), F32)
    zero_m = jnp.zeros((1, 2, ML_HEADS, 1, LANES), F32)

    u = _modulate(x, mod[0], n_p, dseq)
    states = []
    for l in range(depth):
        proj = _inproj(u, w_in_bf[l])
        g, gt = _gates(u, wg_pad[l], wgt[l], gb_row[l], gb_col[l])
        g_cols = g[:, :N_GATES].reshape(n_p + n_s, 4, ML_HEADS).transpose(2, 0, 1)
        g_rows = gt.reshape(4, ML_HEADS, n_p + n_s).transpose(1, 0, 2)

        conv_b = rg_conv_b[l].reshape(1, D_RG)
        rg_p, ht_p = _rglru(proj, 0, bsz, seq, seq, rg_conv_w[l], conv_b, rg_wg[l], rg_bg[l], rg_lambda[l], zero_rg)
        rg_s, _ = _rglru(proj, n_p // dseq, dbs, dseq, GRID_W, rg_conv_w[l], conv_b, rg_wg[l], rg_bg[l],
                         rg_lambda[l], state_rglru[:, l])

        norm_w = ml_norm_w[l].reshape(1, D_ML)
        ml_p, c_p, nn_p, m_p = _mlstm(proj, 0, bsz, seq, g_cols, g_rows, zero_c, zero_n, zero_m, norm_w, True)
        m0_s = jnp.broadcast_to(state_mlstm_m[:, l][..., None, None], (dbs, 2, ML_HEADS, 1, LANES))
        (ml_s,) = _mlstm(proj, n_p // dseq, dbs, dseq, g_cols, g_rows, state_mlstm_C[:, l],
                         state_mlstm_n[:, l][:, :, :, None, :], m0_s, norm_w, False)
        states.append((ht_p, c_p, nn_p[:, :, :, 0, :], m_p[:, :, :, 0, 0]))

        rg_y = jnp.concatenate([rg_p, rg_s], axis=0)
        ml_y = jnp.concatenate([ml_p, ml_s], axis=0)
        x1, u2p, ti, tg = _outproj(rg_y, ml_y, w_out_bf[l], x, mod[l], ln1_g[l].reshape(1, d), ln1_b[l].reshape(1, d),
                                  rw_pad[l], rb_pad[l], alpha, n_experts, n_p, dseq)
        y4p = _moe(u2p, ti[:, :TOP_K], l, moe_w1, moe_b1, moe_w2, moe_b2)
        x, u = _combine(x1, y4p, tg, mod[l], mod[min(l + 1, depth - 1)], ln2_g[l].reshape(1, d),
                        ln2_b[l].reshape(1, d), alpha, n_p, dseq)

    new_rglru = jnp.stack([s[0] for s in states], axis=1)
    new_c = jnp.stack([s[1] for s in states], axis=1)
    new_n = jnp.stack([s[2] for s in states], axis=1)
    new_m = jnp.stack([s[3] for s in states], axis=1)
    return (x[:n_p].reshape(bsz, seq, d), x[n_p:].reshape(dbs, dseq, d), new_rglru, new_c, new_n, new_m)
```

```python
import functools

import jax
import jax.numpy as jnp
from jax import lax
from jax.experimental import pallas as pl
from jax.experimental.pallas import tpu as pltpu

F32 = jnp.float32
BF16 = jnp.bfloat16

D_MODEL = 2048
D_RG = 1024
RG_BLOCK_W = 64
RG_CONV_W = 4
RG_C = 8.0
GRID_W = 64
D_ML = 1024
ML_HEADS = 4
ML_HD = 256
TOP_K = 4
SWIGLU_LIMIT = 7.0
SWIGLU_ALPHA = 1.702
LN_EPS = 1e-5
N_MOD = 6
CHUNK = 256
RG_SCAN_UNROLL = 4
OUTPROJ_SUBTILES = 2
SUBLANES = 8
LANES = 128
N_GATES = 4 * ML_HEADS
COL_Q = 2 * D_RG
COL_K = COL_Q + D_ML
COL_V = COL_K + D_ML
COL_O = COL_V + D_ML
D_PROJ = COL_O + D_ML

TOKEN_TILE_ROWS = D_MODEL // 2 // LANES
MOE_CAP = 1280
MOE_RB = 128
MOE_TF = 256
MOE_DMA_UNROLL = 16
MOE_SIDE_QUANTA = 16
MOE_QUANTUM = MOE_CAP // MOE_SIDE_QUANTA
MOE_W_SLOTS = 3
VMEM_LIMIT = 56 * 1024 * 1024
HIGHEST = lax.Precision.HIGHEST


def _pack_bf16_pair(lo, hi):
    lo_bits = lax.bitcast_convert_type(lo.astype(BF16).astype(F32), jnp.uint32)
    hi_bits = lax.bitcast_convert_type(hi.astype(BF16).astype(F32), jnp.uint32)
    return (lo_bits >> 16) | (hi_bits & jnp.uint32(0xFFFF0000))


def _unpack_bf16_pair(w):
    lo = lax.bitcast_convert_type(w << 16, F32)
    hi = lax.bitcast_convert_type(w & jnp.uint32(0xFFFF0000), F32)
    return lo, hi


def _sigmoid(x):
    return 1.0 / (1.0 + jnp.exp(-x))


def _softplus(x):
    return jnp.maximum(x, 0.0) + jnp.log1p(jnp.exp(-jnp.abs(x)))


def _gelu_tanh(x):
    return 0.5 * x * (1.0 + jnp.tanh(0.7978845608028654 * (x + 0.044715 * (x * x * x))))


def _layer_norm(z, g, b):
    mu = jnp.mean(z, axis=-1, keepdims=True)
    zc = z - mu
    var = jnp.mean(zc * zc, axis=-1, keepdims=True)
    return zc * lax.rsqrt(var + LN_EPS) * g + b


def _params(*sem):
    return pltpu.CompilerParams(dimension_semantics=sem, vmem_limit_bytes=VMEM_LIMIT)


def _cond_index(row_start, n_p, ds):
    return jnp.where(row_start < n_p, 0, 1 + (row_start - n_p) // ds)


def _ada_body(c_ref, w_ref, b_ref, o_ref):
    c = c_ref[...]
    s = c * _sigmoid(c)
    o_ref[...] = jnp.dot(s, w_ref[...], preferred_element_type=F32, precision=HIGHEST) + b_ref[...]


def _ada(conds, w_ada, b_ada, tn=1024):
    depth, d, n6 = w_ada.shape
    nc = conds.shape[0]
    return pl.pallas_call(
        _ada_body,
        out_shape=jax.ShapeDtypeStruct((depth, nc, n6), F32),
        grid=(depth, n6 // tn),
        in_specs=[
            pl.BlockSpec((nc, d), lambda l, j: (0, 0)),
            pl.BlockSpec((None, d, tn), lambda l, j: (l, 0, j)),
            pl.BlockSpec((None, 1, tn), lambda l, j: (l, 0, j)),
        ],
        out_specs=pl.BlockSpec((None, nc, tn), lambda l, j: (l, 0, j)),
        compiler_params=_params("arbitrary", "arbitrary"),
        name="ada",
    )(conds, w_ada, b_ada.reshape(depth, 1, n6))


def _modulate_body(x_ref, mod_ref, u_ref):
    u_ref[...] = (x_ref[...] * (1.0 + mod_ref[1:2, :]) + mod_ref[0:1, :]).astype(BF16)


def _modulate(x, mod_l, n_p, ds, tm=512):
    n, d = x.shape
    return pl.pallas_call(
        _modulate_body,
        out_shape=jax.ShapeDtypeStruct((n, d), BF16),
        grid=(n // tm,),
        in_specs=[
            pl.BlockSpec((tm, d), lambda i: (i, 0)),
            pl.BlockSpec((None, N_MOD, d), lambda i: (_cond_index(i * tm, n_p, ds), 0, 0)),
        ],
        out_specs=pl.BlockSpec((tm, d), lambda i: (i, 0)),
        compiler_params=_params("arbitrary"),
        name="modulate",
    )(x, mod_l)


def _inproj_body(u_ref, w_ref, o_ref):
    o_ref[...] = jnp.dot(u_ref[...], w_ref[...], preferred_element_type=F32)


def _inproj(u, w_bf, tm=1024, tn=768):
    n, d = u.shape
    return pl.pallas_call(
        _inproj_body,
        out_shape=jax.ShapeDtypeStruct((n, D_PROJ), F32),
        grid=(D_PROJ // tn, n // tm),
        in_specs=[
            pl.BlockSpec((tm, d), lambda j, i: (i, 0)),
            pl.BlockSpec((d, tn), lambda j, i: (0, j)),
        ],
        out_specs=pl.BlockSpec((tm, tn), lambda j, i: (i, j)),
        compiler_params=_params("arbitrary", "arbitrary"),
        name="inproj",
    )(u, w_bf)


def _gates_body(u_ref, wg_ref, wgt_ref, brow_ref, bcol_ref, g_ref, gt_ref):
    u = u_ref[...]
    g = jnp.dot(u, wg_ref[...], preferred_element_type=F32) + brow_ref[...]
    lane = lax.broadcasted_iota(jnp.int32, g.shape, 1)
    is_forget = ((lane >> 2) & 1) == 1
    g_ref[...] = jnp.where(is_forget, -_softplus(-g), g)
    gt = lax.dot_general(wgt_ref[...], u, (((1,), (1,)), ((), ())), preferred_element_type=F32) + bcol_ref[...]
    row = lax.broadcasted_iota(jnp.int32, gt.shape, 0)
    is_forget_t = ((row >> 2) & 1) == 1
    gt_ref[...] = jnp.where(is_forget_t, -_softplus(-gt), gt)


def _gates(u, wg_pad, wgt, brow, bcol, tm=1024):
    n, d = u.shape
    return pl.pallas_call(
        _gates_body,
        out_shape=(jax.ShapeDtypeStruct((n, LANES), F32), jax.ShapeDtypeStruct((N_GATES, n), F32)),
        grid=(n // tm,),
        in_specs=[
            pl.BlockSpec((tm, d), lambda i: (i, 0)),
            pl.BlockSpec((d, LANES), lambda i: (0, 0)),
            pl.BlockSpec((N_GATES, d), lambda i: (0, 0)),
            pl.BlockSpec((1, LANES), lambda i: (0, 0)),
            pl.BlockSpec((N_GATES, 1), lambda i: (0, 0)),
        ],
        out_specs=(pl.BlockSpec((tm, LANES), lambda i: (i, 0)), pl.BlockSpec((N_GATES, tm), lambda i: (0, i))),
        compiler_params=_params("arbitrary"),
        name="gates",
    )(u, wg_pad, wgt, brow, bcol)


def _rglru_body(rx_ref, rgate_ref, cw_ref, cb_ref, wg_ref, bg_ref, lam_ref, h0_ref, y_ref, ht_ref,
                af_scr, uf_scr, ab_scr, ub_scr, hf_scr, *, seq_len, period):
    tc = rx_ref.shape[1]
    n_chunks = seq_len // CHUNK
    n_tiles = seq_len // SUBLANES
    sp = _softplus(-lam_ref[...])

    def gate_chunk(c, carry):
        rows = pl.ds(pl.multiple_of(c * CHUNK, CHUNK), CHUNK)
        x = rx_ref[rows, :]
        t = lax.broadcasted_iota(jnp.int32, (CHUNK, 1), 0) & (period - 1)
        xm2 = jnp.where(t >= 2, pltpu.roll(x, 2, 0), 0.0)
        xm1 = jnp.where(t >= 1, pltpu.roll(x, 1, 0), 0.0)
        xp1 = jnp.where(t <= period - 2, pltpu.roll(x, CHUNK - 1, 0), 0.0)
        xc = (cw_ref[0:1, :] * xm2 + cw_ref[1:2, :] * xm1 + cw_ref[2:3, :] * x + cw_ref[3:4, :] * xp1
              + cb_ref[...])
        g = jnp.dot(xc.astype(BF16), wg_ref[...], preferred_element_type=F32) + bg_ref[...]
        for d, (a_scr, u_scr) in enumerate(((af_scr, uf_scr), (ab_scr, ub_scr))):
            r = _sigmoid(g[:, (2 * d) * tc:(2 * d + 1) * tc])
            i = _sigmoid(g[:, (2 * d + 1) * tc:(2 * d + 2) * tc])
            log_a = (-RG_C) * r * sp[d:d + 1, :]
            a = jnp.exp(log_a)
            a_scr[rows, :] = a
            u_scr[rows, :] = jnp.sqrt(-jnp.tanh(log_a) * (a * a + 1.0)) * (i * xc)
        return carry

    lax.fori_loop(0, n_chunks, gate_chunk, 0)

    sub = lax.broadcasted_iota(jnp.int32, (SUBLANES, 1), 0)

    def tile_scan(a, u, reverse):
        for d in (1, 2, 4):
            if reverse:
                valid = sub < SUBLANES - d
                shift = SUBLANES - d
            else:
                valid = sub >= d
                shift = d
            a_sh = pltpu.roll(a, shift, 0)
            u_sh = pltpu.roll(u, shift, 0)
            u = jnp.where(valid, a * u_sh + u, u)
            a = jnp.where(valid, a * a_sh, a)
        return a, u

    def fwd(i, h_prev):
        rows = pl.ds(pl.multiple_of(i * SUBLANES, SUBLANES), SUBLANES)
        a, u = tile_scan(af_scr[rows, :], uf_scr[rows, :], False)
        h = a * h_prev + u
        hf_scr[rows, :] = h
        return jnp.broadcast_to(h[SUBLANES - 1:SUBLANES, :], h.shape)

    h0f = jnp.broadcast_to(h0_ref[0:1, :], (SUBLANES, tc))
    h_last = lax.fori_loop(0, n_tiles, fwd, h0f, unroll=RG_SCAN_UNROLL)
    ht_ref[0:1, :] = h_last[0:1, :]

    def bwd(k, h_next):
        i = n_tiles - 1 - k
        rows = pl.ds(pl.multiple_of(i * SUBLANES, SUBLANES), SUBLANES)
        a, u = tile_scan(ab_scr[rows, :], ub_scr[rows, :], True)
        h = a * h_next + u
        y = (hf_scr[rows, :] + h) * _gelu_tanh(rgate_ref[rows, :])
        y_ref[rows, :] = y.astype(y_ref.dtype)
        return jnp.broadcast_to(h[0:1, :], h.shape)

    h0b = jnp.broadcast_to(h0_ref[1:2, :], (SUBLANES, tc))
    h_first = lax.fori_loop(0, n_tiles, bwd, h0b, unroll=RG_SCAN_UNROLL)
    ht_ref[1:2, :] = h_first[0:1, :]


def _rglru(proj, row_block0, n_seq, seq_len, period, conv_w, conv_b, wg, bg, lam, h0, tc=256):
    n_ct = D_RG // tc
    body = functools.partial(_rglru_body, seq_len=seq_len, period=period)
    h0_map = (lambda s, g: (s, 0, g)) if h0.shape[0] > 1 else (lambda s, g: (0, 0, g))
    return pl.pallas_call(
        body,
        out_shape=(jax.ShapeDtypeStruct((n_seq * seq_len, D_RG), BF16),
                   jax.ShapeDtypeStruct((n_seq, 2, D_RG), F32)),
        grid=(n_seq, n_ct),
        in_specs=[
            pl.BlockSpec((seq_len, tc), lambda s, g: (row_block0 + s, g)),
            pl.BlockSpec((seq_len, tc), lambda s, g: (row_block0 + s, n_ct + g)),
            pl.BlockSpec((RG_CONV_W, tc), lambda s, g: (0, g)),
            pl.BlockSpec((1, tc), lambda s, g: (0, g)),
            pl.BlockSpec((None, tc, 4 * tc), lambda s, g: (g, 0, 0)),
            pl.BlockSpec((None, 1, 4 * tc), lambda s, g: (g, 0, 0)),
            pl.BlockSpec((2, tc), lambda s, g: (0, g)),
            pl.BlockSpec((None, 2, tc), h0_map),
        ],
        out_specs=(pl.BlockSpec((seq_len, tc), lambda s, g: (s, g)),
                   pl.BlockSpec((None, 2, tc), lambda s, g: (s, 0, g))),
        scratch_shapes=[pltpu.VMEM((seq_len, tc), F32) for _ in range(5)],
        compiler_params=_params("arbitrary", "arbitrary"),
        name="rglru",
    )(proj, proj, conv_w, conv_b, wg, bg, lam, h0)


def _mlstm_body(q_ref, k_ref, v_ref, og_ref, gc_ref, gr_ref, c0_ref, n0_ref, m0_ref, nw_ref,
                y_ref, *rest, seq_len, with_state):
    if with_state:
        c_out, n_out, m_out, hf_scr, c_scr, n_scr, m_scr = rest
    else:
        hf_scr, c_scr, n_scr, m_scr = rest
    L = CHUNK
    n_chunks = seq_len // L
    ii = lax.broadcasted_iota(jnp.int32, (L, L), 0)
    jj = lax.broadcasted_iota(jnp.int32, (L, L), 1)
    scale = ML_HD ** -0.5

    def chunk_step(rows, direction, update_state):
        mask = (jj <= ii) if direction == 0 else (jj >= ii)
        mask_t = (ii <= jj) if direction == 0 else (ii >= jj)
        ig_r = gr_ref[2 * direction:2 * direction + 1, rows]
        lf_r = gr_ref[2 * direction + 1:2 * direction + 2, rows]
        ig_c = gc_ref[rows, 2 * direction:2 * direction + 1]
        lf_c = gc_ref[rows, 2 * direction + 1:2 * direction + 2]
        b_col = jnp.sum(jnp.where(mask, lf_r, 0.0), axis=1, keepdims=True)
        b_row = jnp.sum(jnp.where(mask_t, lf_c, 0.0), axis=0, keepdims=True)
        total = jnp.sum(lf_r, axis=1, keepdims=True)
        m_prev = m_scr[0:1, 0:1]
        dmat = jnp.where(mask, b_col - b_row + ig_r, -jnp.inf)
        inter = b_col + m_prev
        m_t = jnp.maximum(inter, jnp.max(dmat, axis=1, keepdims=True))
        w_inter = jnp.exp(inter - m_t)
        q = q_ref[rows, :]
        k = k_ref[rows, :] * scale
        v = v_ref[rows, :]
        qb, kb, vb = q.astype(BF16), k.astype(BF16), v.astype(BF16)
        qk = lax.dot_general(qb, kb, (((1,), (1,)), ((), ())), preferred_element_type=F32)
        s = qk * jnp.exp(dmat - m_t)
        c_prev = c_scr[...]
        n_prev = n_scr[...]
        num = (w_inter * jnp.dot(qb, c_prev.astype(BF16), preferred_element_type=F32)
               + jnp.dot(s.astype(BF16), vb, preferred_element_type=F32))
        den = w_inter * jnp.sum(q * n_prev, axis=1, keepdims=True) + jnp.sum(s, axis=1, keepdims=True)
        h = num / jnp.maximum(jnp.abs(den), jnp.exp(-m_t))
        if update_state:
            g_row = total - b_row + ig_r
            g_col = total - b_col + ig_c
            m_new = jnp.maximum(total + m_prev, jnp.max(g_row, axis=1, keepdims=True))
            decay = jnp.exp(total + m_prev - m_new)
            kw = k * jnp.exp(g_col - m_new)
            c_scr[...] = decay * c_prev + lax.dot_general(
                kw.astype(BF16), vb, (((0,), (0,)), ((), ())), preferred_element_type=F32)
            n_scr[...] = decay * n_prev + jnp.sum(kw, axis=0, keepdims=True)
            m_scr[...] = jnp.broadcast_to(m_new, m_scr.shape)
        return h

    for direction in (0, 1):
        c_scr[...] = c0_ref[direction]
        n_scr[...] = n0_ref[direction]
        m_scr[...] = m0_ref[direction]

        def body(step, carry, direction=direction):
            c = step if direction == 0 else n_chunks - 1 - step
            rows = pl.ds(pl.multiple_of(c * L, L), L)
            h = chunk_step(rows, direction, with_state or n_chunks > 1)
            if direction == 0:
                hf_scr[rows, :] = h
            else:
                hm = hf_scr[rows, :] + h
                mu = jnp.mean(hm, axis=1, keepdims=True)
                hc = hm - mu
                var = jnp.mean(hc * hc, axis=1, keepdims=True)
                hn = hc * lax.rsqrt(var + LN_EPS) * nw_ref[...]
                y_ref[rows, :] = (_sigmoid(og_ref[rows, :]) * hn).astype(y_ref.dtype)
            return carry

        lax.fori_loop(0, n_chunks, body, 0)
        if with_state:
            c_out[direction] = c_scr[...]
            n_out[direction] = n_scr[...]
            m_out[direction] = m_scr[...]


def _mlstm(proj, row_block0, n_seq, seq_len, g_cols, g_rows, c0, n0, m0, norm_w, with_state):
    hd = ML_HD
    body = functools.partial(_mlstm_body, seq_len=seq_len, with_state=with_state)
    bcast = c0.shape[0] == 1
    smap = (lambda s: 0) if bcast else (lambda s: s)
    col = lambda base: (lambda s, h: (row_block0 + s, base // hd + h))
    out_shape = [jax.ShapeDtypeStruct((n_seq * seq_len, D_ML), BF16)]
    out_specs = [pl.BlockSpec((seq_len, hd), lambda s, h: (s, h))]
    if with_state:
        out_shape += [jax.ShapeDtypeStruct((n_seq, 2, ML_HEADS, hd, hd), F32),
                      jax.ShapeDtypeStruct((n_seq, 2, ML_HEADS, 1, hd), F32),
                      jax.ShapeDtypeStruct((n_seq, 2, ML_HEADS, 1, LANES), F32)]
        out_specs += [pl.BlockSpec((None, 2, None, hd, hd), lambda s, h: (s, 0, h, 0, 0)),
                      pl.BlockSpec((None, 2, None, 1, hd), lambda s, h: (s, 0, h, 0, 0)),
                      pl.BlockSpec((None, 2, None, 1, LANES), lambda s, h: (s, 0, h, 0, 0))]
    return pl.pallas_call(
        body,
        out_shape=tuple(out_shape),
        grid=(n_seq, ML_HEADS),
        in_specs=[
            pl.BlockSpec((seq_len, hd), col(COL_Q)),
            pl.BlockSpec((seq_len, hd), col(COL_K)),
            pl.BlockSpec((seq_len, hd), col(COL_V)),
            pl.BlockSpec((seq_len, hd), col(COL_O)),
            pl.BlockSpec((None, seq_len, 4), lambda s, h: (h, row_block0 + s, 0)),
            pl.BlockSpec((None, 4, seq_len), lambda s, h: (h, 0, row_block0 + s)),
            pl.BlockSpec((None, 2, None, hd, hd), lambda s, h: (smap(s), 0, h, 0, 0)),
            pl.BlockSpec((None, 2, None, 1, hd), lambda s, h: (smap(s), 0, h, 0, 0)),
            pl.BlockSpec((None, 2, None, 1, LANES), lambda s, h: (smap(s), 0, h, 0, 0)),
            pl.BlockSpec((1, hd), lambda s, h: (0, h)),
        ],
        out_specs=tuple(out_specs),
        scratch_shapes=[pltpu.VMEM((seq_len, hd), F32), pltpu.VMEM((hd, hd), F32),
                        pltpu.VMEM((1, hd), F32), pltpu.VMEM((1, LANES), F32)],
        compiler_params=_params("arbitrary", "arbitrary"),
        name="mlstm",
    )(proj, proj, proj, proj, g_cols, g_rows, c0, n0, m0, norm_w)


def _outproj_body(rg_ref, ml_ref, w_ref, x_ref, mod_ref, lng_ref, lnb_ref, rw_ref, rb_ref,
                  x1_ref, u2p_ref, ti_ref, tg_ref, *, alpha, n_experts, n_sub):
    tm = x_ref.shape[0]
    ts = tm // n_sub
    for sub in range(n_sub):
        rows = slice(sub * ts, (sub + 1) * ts)
        mix = (jnp.dot(rg_ref[rows, :], w_ref[0:D_RG, :], preferred_element_type=F32)
               + jnp.dot(ml_ref[rows, :], w_ref[D_RG:D_MODEL, :], preferred_element_type=F32))
        z = alpha * x_ref[rows, :] + mod_ref[2:3, :] * mix
        x1 = _layer_norm(z, lng_ref[...], lnb_ref[...])
        x1_ref[rows, :] = x1
        u2 = x1 * (1.0 + mod_ref[4:5, :]) + mod_ref[3:4, :]
        for s in range(TOKEN_TILE_ROWS):
            lo = u2[:, s * LANES:(s + 1) * LANES]
            hi = u2[:, D_MODEL // 2 + s * LANES:D_MODEL // 2 + (s + 1) * LANES]
            u2p_ref[pl.ds(sub * ts * TOKEN_TILE_ROWS + s, ts, stride=TOKEN_TILE_ROWS), :] = _pack_bf16_pair(lo, hi)
        logits = jnp.dot(u2, rw_ref[...], preferred_element_type=F32, precision=HIGHEST) + rb_ref[...]
        lane = lax.broadcasted_iota(jnp.int32, logits.shape, 1)
        logits = jnp.where(lane < n_experts, logits, -jnp.inf)
        vals, idxs = [], []
        for _ in range(TOP_K):
            m = jnp.max(logits, axis=1, keepdims=True)
            idx = jnp.min(jnp.where(logits == m, lane, LANES), axis=1, keepdims=True)
            vals.append(m)
            idxs.append(idx)
            logits = jnp.where(lane == idx, -jnp.inf, logits)
        exps = [jnp.exp(v - vals[0]) for v in vals]
        inv = 1.0 / (exps[0] + exps[1] + exps[2] + exps[3])
        ti = jnp.zeros(lane.shape, jnp.int32)
        tg = jnp.zeros(lane.shape, F32)
        for k in range(TOP_K):
            ti = jnp.where(lane == k, idxs[k], ti)
            tg = jnp.where(lane == k, exps[k] * inv, tg)
        ti_ref[rows, :] = ti
        tg_ref[rows, :] = tg


def _outproj(rg_y, ml_y, w_out_bf, x, mod_l, ln_g, ln_b, rw_pad, rb_pad, alpha, n_experts, n_p, ds, tm=512):
    n, d = x.shape
    body = functools.partial(_outproj_body, alpha=alpha, n_experts=n_experts, n_sub=OUTPROJ_SUBTILES)
    row = lambda i: (i, 0)
    const = lambda i: (0, 0)
    return pl.pallas_call(
        body,
        out_shape=(jax.ShapeDtypeStruct((n, d), F32),
                   jax.ShapeDtypeStruct((n * TOKEN_TILE_ROWS, LANES), jnp.uint32),
                   jax.ShapeDtypeStruct((n, LANES), jnp.int32), jax.ShapeDtypeStruct((n, LANES), F32)),
        grid=(n // tm,),
        in_specs=[
            pl.BlockSpec((tm, D_RG), row),
            pl.BlockSpec((tm, D_ML), row),
            pl.BlockSpec((d, d), const),
            pl.BlockSpec((tm, d), row),
            pl.BlockSpec((None, N_MOD, d), lambda i: (_cond_index(i * tm, n_p, ds), 0, 0)),
            pl.BlockSpec((1, d), const),
            pl.BlockSpec((1, d), const),
            pl.BlockSpec((d, LANES), const),
            pl.BlockSpec((1, LANES), const),
        ],
        out_specs=(pl.BlockSpec((tm, d), row), pl.BlockSpec((tm * TOKEN_TILE_ROWS, LANES), row),
                   pl.BlockSpec((tm, LANES), row), pl.BlockSpec((tm, LANES), row)),
        compiler_params=_params("arbitrary"),
        name="outproj",
    )(rg_y, ml_y, w_out_bf, x, mod_l, ln_g, ln_b, rw_pad, rb_pad)


def _moe_items(top_i, n_experts, max_items):
    n_assign = top_i.size
    e_flat = top_i.reshape(n_assign)
    order = jnp.argsort(e_flat, stable=True).astype(jnp.int32)
    counts = jnp.sum((e_flat[:, None] == jnp.arange(n_experts, dtype=jnp.int32)[None, :]).astype(jnp.int32), axis=0)
    starts = jnp.cumsum(counts) - counts
    items_per_e = (counts + MOE_CAP - 1) // MOE_CAP
    item_ends = jnp.cumsum(items_per_e)
    total = item_ends[-1]
    slot = jnp.arange(max_items, dtype=jnp.int32)
    valid = slot < total
    slot_c = jnp.minimum(slot, total - 1)
    e_s = jnp.minimum(jnp.sum((slot_c[:, None] >= item_ends[None, :]).astype(jnp.int32), axis=1), n_experts - 1)
    local = slot_c - (item_ends[e_s] - items_per_e[e_s])
    start = starts[e_s] + local * MOE_CAP
    n_rows = jnp.clip(counts[e_s] - local * MOE_CAP, 0, MOE_CAP)
    return (order, e_s, jnp.where(valid, start, 0).astype(jnp.int32),
            jnp.where(valid, n_rows, 0).astype(jnp.int32), total.reshape(1).astype(jnp.int32))


def _moe_body(src_row_ref, dst_row_ref, item_e_ref, item_start_ref, item_n_ref, n_items_ref,
              u2p_hbm, w1_hbm, w2_hbm, b1_ref, b2_ref,
              y4p_hbm,
              xq_scr, xb_scr, acc_scr, ys_scr, w1g_f, w1l_f, w2_f, w1g_b, w1l_b, w2_b, gsem, ssem, wsem,
              *, layer, n_assign, n_tokens, n_ft):
    d = acc_scr.shape[1]
    half = d // 2
    de = n_ft * MOE_TF
    tr = TOKEN_TILE_ROWS
    n_items = n_items_ref[0]
    n_steps = n_items * n_ft

    def tile(ref, idx):
        return ref.at[pl.ds(pl.multiple_of(idx * tr, tr), tr)]

    def rows8(ref, row):
        return ref.at[pl.ds(pl.multiple_of(row, tr), tr)]

    def granules(n_rows):
        return (n_rows + MOE_RB - 1) // MOE_RB

    def gather_quantum(it, k, lo=0, hi=MOE_QUANTUM):
        base = item_start_ref[it] + k * MOE_QUANTUM
        for i in range(lo, hi):
            src = src_row_ref[jnp.minimum(base + i, n_assign - 1)]
            pltpu.make_async_copy(rows8(u2p_hbm, src), tile(xq_scr, k * MOE_QUANTUM + i),
                                  gsem).start(priority=i % 2)

    def scatter_quantum(it, n_valid, k, lo=0, hi=MOE_QUANTUM):
        base = item_start_ref[it] + k * MOE_QUANTUM
        for i in range(lo, hi):
            r = k * MOE_QUANTUM + i
            real = dst_row_ref[jnp.minimum(base + i, n_assign - 1)]
            dst = jnp.where(r < n_valid, real, (TOP_K * n_tokens + r) * tr)
            pltpu.make_async_copy(tile(ys_scr, r), rows8(y4p_hbm, dst), ssem).start(priority=i % 2)

    def quanta(lo, hi, fn):
        def f(k, c):
            fn(k)
            return c
        lax.fori_loop(lo, hi, f, 0)

    def wait_tiles(count, src, dst, sem):
        def group(g, c):
            for _ in range(MOE_DMA_UNROLL):
                pltpu.make_async_copy(tile(src, 0), tile(dst, 0), sem).wait()
            return c
        lax.fori_loop(0, count // MOE_DMA_UNROLL, group, 0)

    def weight_copies(step):
        it = step // n_ft
        j = step - it * n_ft
        e = item_e_ref[it]
        slot = lax.rem(step, MOE_W_SLOTS)
        col = pl.multiple_of(j * MOE_TF, MOE_TF)
        col_l = pl.multiple_of(de + j * MOE_TF, MOE_TF)
        return (pltpu.make_async_copy(w1_hbm.at[layer, e, :, pl.ds(col, MOE_TF)], w1g_f.at[slot], wsem.at[slot]),
                pltpu.make_async_copy(w1_hbm.at[layer, e, :, pl.ds(col_l, MOE_TF)], w1l_f.at[slot], wsem.at[slot]),
                pltpu.make_async_copy(w2_hbm.at[layer, e, pl.ds(col, MOE_TF), :], w2_f.at[slot], wsem.at[slot]))

    def cast_part(part, src, dst):
        h = MOE_TF // 2
        if part == 0:
            w1g_b[dst] = w1g_f[src].astype(BF16)
            w2_b[dst, 0:h, :] = w2_f[src, 0:h, :].astype(BF16)
        else:
            w1l_b[dst] = w1l_f[src].astype(BF16)
            w2_b[dst, h:MOE_TF, :] = w2_f[src, h:MOE_TF, :].astype(BF16)

    def compute(row0, m, wb, b1g, b1l):
        rows = pl.ds(row0, m)
        x = xb_scr[rows, :]
        hg = jnp.dot(x, w1g_b[wb], preferred_element_type=F32) + b1g
        hl = jnp.dot(x, w1l_b[wb], preferred_element_type=F32) + b1l
        hg = jnp.minimum(hg, SWIGLU_LIMIT)
        hl = jnp.clip(hl, -SWIGLU_LIMIT, SWIGLU_LIMIT)
        act = hg * _sigmoid(SWIGLU_ALPHA * hg) * (hl + 1.0)
        acc_scr[rows, :] += jnp.dot(act.astype(BF16), w2_b[wb], preferred_element_type=F32)

    for cp in weight_copies(0):
        cp.start()
    for cp in weight_copies(1):
        cp.start()

    def zero_ys(b, c):
        ys_scr[pl.ds(pl.multiple_of(b * (MOE_RB * tr), MOE_RB * tr), MOE_RB * tr), :] = jnp.zeros(
            (MOE_RB * tr, LANES), jnp.uint32)
        return c
    lax.fori_loop(0, MOE_CAP // MOE_RB, zero_ys, 0)
    spare = pltpu.make_async_copy(ys_scr, y4p_hbm.at[pl.ds(TOP_K * n_tokens * tr, MOE_CAP * tr)], ssem)
    spare.start()
    spare.wait()
    for cp in weight_copies(0):
        cp.wait()
    cast_part(0, 0, 0)
    cast_part(1, 0, 0)

    def item_body(it, carry):
        n_prev, g_issued = carry
        it_prev = jnp.maximum(it - 1, 0)
        it_next = jnp.minimum(it + 1, n_items - 1)
        n_rows = item_n_ref[it]
        e = item_e_ref[it]
        n_g = granules(n_rows)
        n_quads = n_g // 4
        q_per = jnp.minimum(n_quads, MOE_SIDE_QUANTA // n_ft)
        quanta(g_issued, MOE_SIDE_QUANTA, lambda k: gather_quantum(it, k))
        wait_tiles(MOE_CAP, u2p_hbm, xq_scr, gsem)

        def unpack(b, c):
            rows = pl.ds(pl.multiple_of(b * MOE_RB, MOE_RB), MOE_RB)
            for t in range(tr):
                lo, hi = _unpack_bf16_pair(xq_scr[pl.ds(b * (MOE_RB * tr) + t, MOE_RB, stride=tr), :])
                xb_scr[rows, t * LANES:(t + 1) * LANES] = lo.astype(BF16)
                xb_scr[rows, half + t * LANES:half + (t + 1) * LANES] = hi.astype(BF16)
            acc_scr[rows, :] = jnp.broadcast_to(b2_ref[pl.ds(e, 1), :], (MOE_RB, d))
            return c
        lax.fori_loop(0, n_g, unpack, 0)

        def tile_step(j, c):
            step = it * n_ft + j
            wb = step & 1
            nxt = lax.rem(step + 1, MOE_W_SLOTS)

            @pl.when(step + 2 < n_steps)
            def _prefetch_weights():
                for cp in weight_copies(step + 2):
                    cp.start()

            @pl.when(step + 1 < n_steps)
            def _next_weights_ready():
                for cp in weight_copies(step + 1):
                    cp.wait()

            b1g = b1_ref[e, pl.ds(j, 1), :]
            b1l = b1_ref[e, pl.ds(n_ft + j, 1), :]
            @pl.when(n_quads >= 2)
            def _eight_granules():
                compute(0, 8 * MOE_RB, wb, b1g, b1l)
                cast_part(0, nxt, 1 - wb)
                cast_part(1, nxt, 1 - wb)
                for k in (2 * j, 2 * j + 1):
                    scatter_quantum(it_prev, n_prev, k)
                    gather_quantum(it_next, k)

            @pl.when(n_quads == 1)
            def _four_granules():
                compute(0, 4 * MOE_RB, wb, b1g, b1l)
                cast_part(0, nxt, 1 - wb)
                cast_part(1, nxt, 1 - wb)
                scatter_quantum(it_prev, n_prev, j)
                gather_quantum(it_next, j)

            @pl.when(n_quads == 0)
            def _no_big_chunk():
                cast_part(0, nxt, 1 - wb)
                cast_part(1, nxt, 1 - wb)

            @pl.when((n_g & 2) != 0)
            def _two():
                compute(pl.multiple_of(n_quads * (4 * MOE_RB), 2 * MOE_RB), 2 * MOE_RB, wb, b1g, b1l)

            @pl.when((n_g & 1) != 0)
            def _one():
                compute(pl.multiple_of((n_g - 1) * MOE_RB, MOE_RB), MOE_RB, wb, b1g, b1l)
            return c
        lax.fori_loop(0, n_ft, tile_step, 0)

        issued = n_ft * q_per
        quanta(issued, MOE_SIDE_QUANTA, lambda k: scatter_quantum(it_prev, n_prev, k))
        wait_tiles(MOE_CAP, ys_scr, y4p_hbm, ssem)

        def pack(b, c):
            rows = pl.ds(pl.multiple_of(b * MOE_RB, MOE_RB), MOE_RB)
            for t in range(tr):
                lo = acc_scr[rows, t * LANES:(t + 1) * LANES]
                hi = acc_scr[rows, half + t * LANES:half + (t + 1) * LANES]
                ys_scr[pl.ds(b * (MOE_RB * tr) + t, MOE_RB, stride=tr), :] = _pack_bf16_pair(lo, hi)
            return c
        lax.fori_loop(0, n_g, pack, 0)
        return n_rows, issued

    n_last, g_extra = lax.fori_loop(0, n_items, item_body, (jnp.int32(0), jnp.int32(0)))
    wait_tiles(g_extra * MOE_QUANTUM, u2p_hbm, xq_scr, gsem)
    quanta(0, MOE_SIDE_QUANTA, lambda k: scatter_quantum(n_items - 1, n_last, k))
    wait_tiles(MOE_CAP, ys_scr, y4p_hbm, ssem)


def _moe(u2p, top_i, layer, w1, b1, w2, b2):
    n = u2p.shape[0] // TOKEN_TILE_ROWS
    depth, n_experts, d, two_de = w1.shape
    de = two_de // 2
    n_ft = de // MOE_TF
    n_assign = n * TOP_K
    max_items = n_experts + n_assign // MOE_CAP
    order, item_e, item_start, item_n, n_items = _moe_items(top_i, n_experts, max_items)
    src_row = (order // TOP_K) * TOKEN_TILE_ROWS
    dst_row = ((order % TOP_K) * n + order // TOP_K) * TOKEN_TILE_ROWS
    assert MOE_SIDE_QUANTA % n_ft == 0 and MOE_SIDE_QUANTA // n_ft >= MOE_CAP // (4 * MOE_RB)
    body = functools.partial(_moe_body, layer=layer, n_assign=n_assign, n_tokens=n, n_ft=n_ft)
    tiles = MOE_CAP * TOKEN_TILE_ROWS
    grid_spec = pltpu.PrefetchScalarGridSpec(
        num_scalar_prefetch=6,
        grid=(1,),
        in_specs=[
            pl.BlockSpec(memory_space=pl.ANY),
            pl.BlockSpec(memory_space=pl.ANY),
            pl.BlockSpec(memory_space=pl.ANY),
            pl.BlockSpec((None, n_experts, 2 * n_ft, MOE_TF), lambda i, *_: (layer, 0, 0, 0)),
            pl.BlockSpec((None, n_experts, d), lambda i, *_: (layer, 0, 0)),
        ],
        out_specs=pl.BlockSpec(memory_space=pl.ANY),
        scratch_shapes=[
            pltpu.VMEM((tiles, LANES), jnp.uint32), pltpu.VMEM((MOE_CAP, d), BF16),
            pltpu.VMEM((MOE_CAP, d), F32), pltpu.VMEM((tiles, LANES), jnp.uint32),
            pltpu.VMEM((MOE_W_SLOTS, d, MOE_TF), F32), pltpu.VMEM((MOE_W_SLOTS, d, MOE_TF), F32),
            pltpu.VMEM((MOE_W_SLOTS, MOE_TF, d), F32),
            pltpu.VMEM((2, d, MOE_TF), BF16), pltpu.VMEM((2, d, MOE_TF), BF16), pltpu.VMEM((2, MOE_TF, d), BF16),
            pltpu.SemaphoreType.DMA, pltpu.SemaphoreType.DMA, pltpu.SemaphoreType.DMA((MOE_W_SLOTS,)),
        ],
    )
    return pl.pallas_call(
        body,
        out_shape=jax.ShapeDtypeStruct(((n_assign + MOE_CAP) * TOKEN_TILE_ROWS, LANES), jnp.uint32),
        grid_spec=grid_spec,
        compiler_params=_params("arbitrary"),
        name="moe",
    )(src_row, dst_row, item_e, item_start, item_n, n_items, u2p, w1, w2,
      b1.reshape(depth, n_experts, 2 * n_ft, MOE_TF), b2)


def _combine_body(x1_ref, y0_ref, y1_ref, y2_ref, y3_ref, tg_ref, mod_ref, modn_ref, lng_ref, lnb_ref,
                  x2_ref, un_ref, ff_scr, *, alpha):
    tm, d = x1_ref.shape
    half = d // 2
    tr = TOKEN_TILE_ROWS
    tg = tg_ref[...]
    for t in range(tr):
        lo_sum = hi_sum = None
        for k, y_ref in enumerate((y0_ref, y1_ref, y2_ref, y3_ref)):
            lo, hi = _unpack_bf16_pair(y_ref[pl.ds(t, tm, stride=tr), :])
            g = tg[:, k:k + 1]
            lo_sum = g * lo if lo_sum is None else lo_sum + g * lo
            hi_sum = g * hi if hi_sum is None else hi_sum + g * hi
        ff_scr[:, t * LANES:(t + 1) * LANES] = lo_sum
        ff_scr[:, half + t * LANES:half + (t + 1) * LANES] = hi_sum
    z = alpha * x1_ref[...] + mod_ref[5:6, :] * ff_scr[...]
    x2 = _layer_norm(z, lng_ref[...], lnb_ref[...])
    x2_ref[...] = x2
    un_ref[...] = (x2 * (1.0 + modn_ref[1:2, :]) + modn_ref[0:1, :]).astype(BF16)


def _combine(x1, y4p, tg, mod_l, mod_next, ln_g, ln_b, alpha, n_p, ds, tm=256):
    n, d = x1.shape
    body = functools.partial(_combine_body, alpha=alpha)
    row = lambda i: (i, 0)
    const = lambda i: (0, 0)
    cond = lambda i: (_cond_index(i * tm, n_p, ds), 0, 0)
    y_specs = [pl.BlockSpec((tm * TOKEN_TILE_ROWS, LANES), lambda i, k=k: (k * (n // tm) + i, 0))
               for k in range(TOP_K)]
    return pl.pallas_call(
        body,
        out_shape=(jax.ShapeDtypeStruct((n, d), F32), jax.ShapeDtypeStruct((n, d), BF16)),
        grid=(n // tm,),
        in_specs=[
            pl.BlockSpec((tm, d), row),
            *y_specs,
            pl.BlockSpec((tm, LANES), row),
            pl.BlockSpec((None, N_MOD, d), cond),
            pl.BlockSpec((None, N_MOD, d), cond),
            pl.BlockSpec((1, d), const),
            pl.BlockSpec((1, d), const),
        ],
        out_specs=(pl.BlockSpec((tm, d), row), pl.BlockSpec((tm, d), row)),
        scratch_shapes=[pltpu.VMEM((tm, d), F32)],
        compiler_params=_params("arbitrary"),
        name="combine",
    )(x1, y4p, y4p, y4p, y4p, tg, mod_l, mod_next, ln_g, ln_b)


def _block_diag_tiles(w, tc):
    depth, nb, bw, _ = w.shape
    per = tc // bw
    wt = w.reshape(depth, nb // per, per, bw, bw)
    on_diag = jnp.eye(per, dtype=bool)[None, None, :, None, :, None]
    full = jnp.where(on_diag, wt[:, :, :, :, None, :], jnp.zeros((), w.dtype))
    return full.reshape(depth, nb // per, tc, tc)


def _rg_gate_weights(rg_wa, rg_ba, rg_wx, rg_bx, tc=256):
    depth = rg_wa.shape[0]
    n_ct = D_RG // tc
    tiles = [_block_diag_tiles(w[:, d], tc) for d in (0, 1) for w in (rg_wa, rg_wx)]
    wg = jnp.concatenate(tiles, axis=-1).astype(BF16)
    biases = [b[:, d].reshape(depth, n_ct, 1, tc) for d in (0, 1) for b in (rg_ba, rg_bx)]
    bg = jnp.concatenate(biases, axis=-1)
    return wg, bg


def kernel(x_prompt, x_sample, state_rglru, state_mlstm_C, state_mlstm_n, state_mlstm_m, c, c_ctx, w_ada, b_ada, w_in, ml_gate_b, rg_conv_w, rg_conv_b, rg_wa, rg_ba, rg_wx, rg_bx, rg_lambda, ml_norm_w, w_out, ln1_g, ln1_b, router_w, router_b, moe_w1, moe_b1, moe_w2, moe_b2, ln2_g, ln2_b):
    depth = w_ada.shape[0]
    bsz, seq, d = x_prompt.shape
    dbs, dseq, _ = x_sample.shape
    n_p, n_s = bsz * seq, dbs * dseq
    n_experts = router_w.shape[-1]
    alpha = float((2 * depth) ** 0.25)
    assert d == D_MODEL and seq == CHUNK and dseq % CHUNK == 0 and n_p % dseq == 0

    x = jnp.concatenate([x_prompt.reshape(n_p, d), x_sample.reshape(n_s, d)], axis=0)
    n_cond = 1 + dbs
    conds = jnp.concatenate([c_ctx[None, :], c, jnp.zeros((SUBLANES - n_cond, d), F32)], axis=0)
    mod = _ada(conds, w_ada, b_ada)
    mod = mod[:, :n_cond].reshape(depth, n_cond, N_MOD, d)

    w_in_bf = w_in.astype(BF16)
    w_g = w_in[:, :, D_PROJ:]
    wg_pad = jnp.pad(w_g, ((0, 0), (0, 0), (0, LANES - N_GATES))).astype(BF16)
    wgt = jnp.swapaxes(w_g, 1, 2).astype(BF16)
    gb_row = jnp.pad(ml_gate_b, ((0, 0), (0, LANES - N_GATES))).reshape(depth, 1, LANES)
    gb_col = ml_gate_b.reshape(depth, N_GATES, 1)
    rg_wg, rg_bg = _rg_gate_weights(rg_wa, rg_ba, rg_wx, rg_bx)
    w_out_bf = w_out.astype(BF16)
    rw_pad = jnp.pad(router_w, ((0, 0), (0, 0), (0, LANES - n_experts)))
    rb_pad = jnp.pad(router_b, ((0, 0), (0, LANES - n_experts))).reshape(depth, 1, LANES)

    zero_rg = jnp.zeros((1, 2, D_RG), F32)
    zero_c = jnp.zeros((1, 2, ML_HEADS, ML_HD, ML_HD), F32)
    zero_n = jnp.zeros((1, 2, ML_HEADS, 1, ML_HD), F32)
    zero_m = jnp.zeros((1, 2, ML_HEADS, 1, LANES), F32)

    u = _modulate(x, mod[0], n_p, dseq)
    states = []
    for l in range(depth):
        proj = _inproj(u, w_in_bf[l])
        g, gt = _gates(u, wg_pad[l], wgt[l], gb_row[l], gb_col[l])
        g_cols = g[:, :N_GATES].reshape(n_p + n_s, 4, ML_HEADS).transpose(2, 0, 1)
        g_rows = gt.reshape(4, ML_HEADS, n_p + n_s).transpose(1, 0, 2)

        conv_b = rg_conv_b[l].reshape(1, D_RG)
        rg_p, ht_p = _rglru(proj, 0, bsz, seq, seq, rg_conv_w[l], conv_b, rg_wg[l], rg_bg[l], rg_lambda[l], zero_rg)
        rg_s, _ = _rglru(proj, n_p // dseq, dbs, dseq, GRID_W, rg_conv_w[l], conv_b, rg_wg[l], rg_bg[l],
                         rg_lambda[l], state_rglru[:, l])

        norm_w = ml_norm_w[l].reshape(1, D_ML)
        ml_p, c_p, nn_p, m_p = _mlstm(proj, 0, bsz, seq, g_cols, g_rows, zero_c, zero_n, zero_m, norm_w, True)
        m0_s = jnp.broadcast_to(state_mlstm_m[:, l][..., None, None], (dbs, 2, ML_HEADS, 1, LANES))
        (ml_s,) = _mlstm(proj, n_p // dseq, dbs, dseq, g_cols, g_rows, state_mlstm_C[:, l],
                         state_mlstm_n[:, l][:, :, :, None, :], m0_s, norm_w, False)
        states.append((ht_p, c_p, nn_p[:, :, :, 0, :], m_p[:, :, :, 0, 0]))

        rg_y = jnp.concatenate([rg_p, rg_s], axis=0)
        ml_y = jnp.concatenate([ml_p, ml_s], axis=0)
        x1, u2p, ti, tg = _outproj(rg_y, ml_y, w_out_bf[l], x, mod[l], ln1_g[l].reshape(1, d), ln1_b[l].reshape(1, d),
                                  rw_pad[l], rb_pad[l], alpha, n_experts, n_p, dseq)
        y4p = _moe(u2p, ti[:, :TOP_K], l, moe_w1, moe_b1, moe_w2, moe_b2)
        x, u = _combine(x1, y4p, tg, mod[l], mod[min(l + 1, depth - 1)], ln2_g[l].reshape(1, d),
                        ln2_b[l].reshape(1, d), alpha, n_p, dseq)

    new_rglru = jnp.stack([s[0] for s in states], axis=1)
    new_c = jnp.stack([s[1] for s in states], axis=1)
    new_n = jnp.stack([s[2] for s in states], axis=1)
    new_m = jnp.stack([s[3] for s in states], axis=1)
    return (x[:n_p].reshape(bsz, seq, d), x[n_p:].reshape(dbs, dseq, d), new_rglru, new_c, new_n, new_m)
```

```python
import functools

import jax
import jax.numpy as jnp
from jax import lax
from jax.experimental import pallas as pl
from jax.experimental.pallas import tpu as pltpu

F32 = jnp.float32
BF16 = jnp.bfloat16

D_MODEL = 2048
D_RG = 1024
RG_BLOCK_W = 64
RG_CONV_W = 4
RG_C = 8.0
GRID_W = 64
D_ML = 1024
ML_HEADS = 4
ML_HD = 256
TOP_K = 4
SWIGLU_LIMIT = 7.0
SWIGLU_ALPHA = 1.702
LN_EPS = 1e-5
N_MOD = 6
CHUNK = 256
RG_SCAN_UNROLL = 4
OUTPROJ_SUBTILES = 2
SUBLANES = 8
LANES = 128
N_GATES = 4 * ML_HEADS
COL_Q = 2 * D_RG
COL_K = COL_Q + D_ML
COL_V = COL_K + D_ML
COL_O = COL_V + D_ML
D_PROJ = COL_O + D_ML

TOKEN_TILE_ROWS = D_MODEL // 2 // LANES
MOE_CAP = 1280
MOE_RB = 128
MOE_TF = 256
MOE_DMA_UNROLL = 16
MOE_SIDE_QUANTA = 16
MOE_QUANTUM = MOE_CAP // MOE_SIDE_QUANTA
MOE_W_SLOTS = 3
VMEM_LIMIT = 56 * 1024 * 1024
HIGHEST = lax.Precision.HIGHEST


def _pack_bf16_pair(lo, hi):
    lo_bits = lax.bitcast_convert_type(lo.astype(BF16).astype(F32), jnp.uint32)
    hi_bits = lax.bitcast_convert_type(hi.astype(BF16).astype(F32), jnp.uint32)
    return (lo_bits >> 16) | (hi_bits & jnp.uint32(0xFFFF0000))


def _unpack_bf16_pair(w):
    lo = lax.bitcast_convert_type(w << 16, F32)
    hi = lax.bitcast_convert_type(w & jnp.uint32(0xFFFF0000), F32)
    return lo, hi


def _sigmoid(x):
    return 0.5 * jnp.tanh(0.5 * x) + 0.5


def _softplus(x):
    return jnp.maximum(x, 0.0) + jnp.log1p(jnp.exp(-jnp.abs(x)))


def _gelu_tanh(x):
    return 0.5 * x * (1.0 + jnp.tanh(0.7978845608028654 * (x + 0.044715 * (x * x * x))))


def _layer_norm(z, g, b):
    mu = jnp.mean(z, axis=-1, keepdims=True)
    zc = z - mu
    var = jnp.mean(zc * zc, axis=-1, keepdims=True)
    return zc * lax.rsqrt(var + LN_EPS) * g + b


def _params(*sem):
    return pltpu.CompilerParams(dimension_semantics=sem, vmem_limit_bytes=VMEM_LIMIT)


def _cond_index(row_start, n_p, ds):
    return jnp.where(row_start < n_p, 0, 1 + (row_start - n_p) // ds)


def _ada_body(c_ref, w_ref, b_ref, o_ref):
    c = c_ref[...]
    s = c * _sigmoid(c)
    o_ref[...] = jnp.dot(s, w_ref[...], preferred_element_type=F32, precision=HIGHEST) + b_ref[...]


def _ada(conds, w_ada, b_ada, tn=1024):
    depth, d, n6 = w_ada.shape
    nc = conds.shape[0]
    return pl.pallas_call(
        _ada_body,
        out_shape=jax.ShapeDtypeStruct((depth, nc, n6), F32),
        grid=(depth, n6 // tn),
        in_specs=[
            pl.BlockSpec((nc, d), lambda l, j: (0, 0)),
            pl.BlockSpec((None, d, tn), lambda l, j: (l, 0, j)),
            pl.BlockSpec((None, 1, tn), lambda l, j: (l, 0, j)),
        ],
        out_specs=pl.BlockSpec((None, nc, tn), lambda l, j: (l, 0, j)),
        compiler_params=_params("arbitrary", "arbitrary"),
        name="ada",
    )(conds, w_ada, b_ada.reshape(depth, 1, n6))


def _modulate_body(x_ref, mod_ref, u_ref):
    u_ref[...] = (x_ref[...] * (1.0 + mod_ref[1:2, :]) + mod_ref[0:1, :]).astype(BF16)


def _modulate(x, mod_l, n_p, ds, tm=512):
    n, d = x.shape
    return pl.pallas_call(
        _modulate_body,
        out_shape=jax.ShapeDtypeStruct((n, d), BF16),
        grid=(n // tm,),
        in_specs=[
            pl.BlockSpec((tm, d), lambda i: (i, 0)),
            pl.BlockSpec((None, N_MOD, d), lambda i: (_cond_index(i * tm, n_p, ds), 0, 0)),
        ],
        out_specs=pl.BlockSpec((tm, d), lambda i: (i, 0)),
        compiler_params=_params("arbitrary"),
        name="modulate",
    )(x, mod_l)


def _inproj_body(u_ref, w_ref, o_ref):
    o_ref[...] = jnp.dot(u_ref[...], w_ref[...], preferred_element_type=F32)


def _inproj(u, w_bf, tm=1024, tn=768):
    n, d = u.shape
    return pl.pallas_call(
        _inproj_body,
        out_shape=jax.ShapeDtypeStruct((n, D_PROJ), F32),
        grid=(D_PROJ // tn, n // tm),
        in_specs=[
            pl.BlockSpec((tm, d), lambda j, i: (i, 0)),
            pl.BlockSpec((d, tn), lambda j, i: (0, j)),
        ],
        out_specs=pl.BlockSpec((tm, tn), lambda j, i: (i, j)),
        compiler_params=_params("arbitrary", "arbitrary"),
        name="inproj",
    )(u, w_bf)


def _gates_body(u_ref, wg_ref, wgt_ref, brow_ref, bcol_ref, g_ref, gt_ref):
    u = u_ref[...]
    g = jnp.dot(u, wg_ref[...], preferred_element_type=F32) + brow_ref[...]
    lane = lax.broadcasted_iota(jnp.int32, g.shape, 1)
    is_forget = ((lane >> 2) & 1) == 1
    g_ref[...] = jnp.where(is_forget, -_softplus(-g), g)
    gt = lax.dot_general(wgt_ref[...], u, (((1,), (1,)), ((), ())), preferred_element_type=F32) + bcol_ref[...]
    row = lax.broadcasted_iota(jnp.int32, gt.shape, 0)
    is_forget_t = ((row >> 2) & 1) == 1
    gt_ref[...] = jnp.where(is_forget_t, -_softplus(-gt), gt)


def _gates(u, wg_pad, wgt, brow, bcol, tm=1024):
    n, d = u.shape
    return pl.pallas_call(
        _gates_body,
        out_shape=(jax.ShapeDtypeStruct((n, LANES), F32), jax.ShapeDtypeStruct((N_GATES, n), F32)),
        grid=(n // tm,),
        in_specs=[
            pl.BlockSpec((tm, d), lambda i: (i, 0)),
            pl.BlockSpec((d, LANES), lambda i: (0, 0)),
            pl.BlockSpec((N_GATES, d), lambda i: (0, 0)),
            pl.BlockSpec((1, LANES), lambda i: (0, 0)),
            pl.BlockSpec((N_GATES, 1), lambda i: (0, 0)),
        ],
        out_specs=(pl.BlockSpec((tm, LANES), lambda i: (i, 0)), pl.BlockSpec((N_GATES, tm), lambda i: (0, i))),
        compiler_params=_params("arbitrary"),
        name="gates",
    )(u, wg_pad, wgt, brow, bcol)


def _rglru_body(rx_ref, rgate_ref, cw_ref, cb_ref, wg_ref, bg_ref, lam_ref, h0_ref, y_ref, ht_ref,
                af_scr, uf_scr, ab_scr, ub_scr, hf_scr, *, seq_len, period):
    tc = rx_ref.shape[1]
    n_chunks = seq_len // CHUNK
    n_tiles = seq_len // SUBLANES
    sp = _softplus(-lam_ref[...])

    def gate_chunk(c, carry):
        rows = pl.ds(pl.multiple_of(c * CHUNK, CHUNK), CHUNK)
        x = rx_ref[rows, :]
        t = lax.broadcasted_iota(jnp.int32, (CHUNK, 1), 0) & (period - 1)
        xm2 = jnp.where(t >= 2, pltpu.roll(x, 2, 0), 0.0)
        xm1 = jnp.where(t >= 1, pltpu.roll(x, 1, 0), 0.0)
        xp1 = jnp.where(t <= period - 2, pltpu.roll(x, CHUNK - 1, 0), 0.0)
        xc = (cw_ref[0:1, :] * xm2 + cw_ref[1:2, :] * xm1 + cw_ref[2:3, :] * x + cw_ref[3:4, :] * xp1
              + cb_ref[...])
        g = jnp.dot(xc.astype(BF16), wg_ref[...], preferred_element_type=F32) + bg_ref[...]
        for d, (a_scr, u_scr) in enumerate(((af_scr, uf_scr), (ab_scr, ub_scr))):
            r = _sigmoid(g[:, (2 * d) * tc:(2 * d + 1) * tc])
            i = _sigmoid(g[:, (2 * d + 1) * tc:(2 * d + 2) * tc])
            log_a = (-RG_C) * r * sp[d:d + 1, :]
            a = jnp.exp(log_a)
            a_scr[rows, :] = a
            u_scr[rows, :] = jnp.sqrt(-jnp.tanh(log_a) * (a * a + 1.0)) * (i * xc)
        return carry

    lax.fori_loop(0, n_chunks, gate_chunk, 0)

    sub = lax.broadcasted_iota(jnp.int32, (SUBLANES, 1), 0)

    def tile_scan(a, u, reverse):
        for d in (1, 2, 4):
            if reverse:
                valid = sub < SUBLANES - d
                shift = SUBLANES - d
            else:
                valid = sub >= d
                shift = d
            a_sh = pltpu.roll(a, shift, 0)
            u_sh = pltpu.roll(u, shift, 0)
            u = jnp.where(valid, a * u_sh + u, u)
            a = jnp.where(valid, a * a_sh, a)
        return a, u

    def fwd(i, h_prev):
        rows = pl.ds(pl.multiple_of(i * SUBLANES, SUBLANES), SUBLANES)
        a, u = tile_scan(af_scr[rows, :], uf_scr[rows, :], False)
        h = a * h_prev + u
        hf_scr[rows, :] = h
        return jnp.broadcast_to(h[SUBLANES - 1:SUBLANES, :], h.shape)

    h0f = jnp.broadcast_to(h0_ref[0:1, :], (SUBLANES, tc))
    h_last = lax.fori_loop(0, n_tiles, fwd, h0f, unroll=RG_SCAN_UNROLL)
    ht_ref[0:1, :] = h_last[0:1, :]

    def bwd(k, h_next):
        i = n_tiles - 1 - k
        rows = pl.ds(pl.multiple_of(i * SUBLANES, SUBLANES), SUBLANES)
        a, u = tile_scan(ab_scr[rows, :], ub_scr[rows, :], True)
        h = a * h_next + u
        y = (hf_scr[rows, :] + h) * _gelu_tanh(rgate_ref[rows, :])
        y_ref[rows, :] = y.astype(y_ref.dtype)
        return jnp.broadcast_to(h[0:1, :], h.shape)

    h0b = jnp.broadcast_to(h0_ref[1:2, :], (SUBLANES, tc))
    h_first = lax.fori_loop(0, n_tiles, bwd, h0b, unroll=RG_SCAN_UNROLL)
    ht_ref[1:2, :] = h_first[0:1, :]


def _rglru(proj, row_block0, n_seq, seq_len, period, conv_w, conv_b, wg, bg, lam, h0, tc=256):
    n_ct = D_RG // tc
    body = functools.partial(_rglru_body, seq_len=seq_len, period=period)
    h0_map = (lambda s, g: (s, 0, g)) if h0.shape[0] > 1 else (lambda s, g: (0, 0, g))
    return pl.pallas_call(
        body,
        out_shape=(jax.ShapeDtypeStruct((n_seq * seq_len, D_RG), BF16),
                   jax.ShapeDtypeStruct((n_seq, 2, D_RG), F32)),
        grid=(n_seq, n_ct),
        in_specs=[
            pl.BlockSpec((seq_len, tc), lambda s, g: (row_block0 + s, g)),
            pl.BlockSpec((seq_len, tc), lambda s, g: (row_block0 + s, n_ct + g)),
            pl.BlockSpec((RG_CONV_W, tc), lambda s, g: (0, g)),
            pl.BlockSpec((1, tc), lambda s, g: (0, g)),
            pl.BlockSpec((None, tc, 4 * tc), lambda s, g: (g, 0, 0)),
            pl.BlockSpec((None, 1, 4 * tc), lambda s, g: (g, 0, 0)),
            pl.BlockSpec((2, tc), lambda s, g: (0, g)),
            pl.BlockSpec((None, 2, tc), h0_map),
        ],
        out_specs=(pl.BlockSpec((seq_len, tc), lambda s, g: (s, g)),
                   pl.BlockSpec((None, 2, tc), lambda s, g: (s, 0, g))),
        scratch_shapes=[pltpu.VMEM((seq_len, tc), F32) for _ in range(5)],
        compiler_params=_params("arbitrary", "arbitrary"),
        name="rglru",
    )(proj, proj, conv_w, conv_b, wg, bg, lam, h0)


def _mlstm_body(q_ref, k_ref, v_ref, og_ref, gc_ref, gr_ref, c0_ref, n0_ref, m0_ref, nw_ref,
                y_ref, *rest, seq_len, with_state):
    if with_state:
        c_out, n_out, m_out, hf_scr, hb_scr, c_scr, n_scr, m_scr = rest
    else:
        hf_scr, hb_scr, c_scr, n_scr, m_scr = rest
    L = CHUNK
    n_chunks = seq_len // L
    ii = lax.broadcasted_iota(jnp.int32, (L, L), 0)
    jj = lax.broadcasted_iota(jnp.int32, (L, L), 1)
    scale = ML_HD ** -0.5

    def chunk_step(rows, direction, update_state):
        mask = (jj <= ii) if direction == 0 else (jj >= ii)
        mask_t = (ii <= jj) if direction == 0 else (ii >= jj)
        ig_r = gr_ref[2 * direction:2 * direction + 1, rows]
        lf_r = gr_ref[2 * direction + 1:2 * direction + 2, rows]
        ig_c = gc_ref[rows, 2 * direction:2 * direction + 1]
        lf_c = gc_ref[rows, 2 * direction + 1:2 * direction + 2]
        b_col = jnp.sum(jnp.where(mask, lf_r, 0.0), axis=1, keepdims=True)
        b_row = jnp.sum(jnp.where(mask_t, lf_c, 0.0), axis=0, keepdims=True)
        total = jnp.sum(lf_r, axis=1, keepdims=True)
        m_prev = m_scr[direction][0:1, 0:1]
        dmat = jnp.where(mask, b_col - b_row + ig_r, -jnp.inf)
        inter = b_col + m_prev
        m_t = jnp.maximum(inter, jnp.max(dmat, axis=1, keepdims=True))
        w_inter = jnp.exp(inter - m_t)
        q = q_ref[rows, :]
        k = k_ref[rows, :] * scale
        v = v_ref[rows, :]
        qb, kb, vb = q.astype(BF16), k.astype(BF16), v.astype(BF16)
        qk = lax.dot_general(qb, kb, (((1,), (1,)), ((), ())), preferred_element_type=F32)
        s = qk * jnp.exp(dmat - m_t)
        sb = s.astype(BF16)
        c_prev = c_scr[direction]
        n_prev = n_scr[direction]
        num = (w_inter * jnp.dot(qb, c_prev.astype(BF16), preferred_element_type=F32)
               + jnp.dot(sb, vb, preferred_element_type=F32))
        n_rows8 = jnp.broadcast_to(n_prev, (SUBLANES, ML_HD)).astype(BF16)
        ones8 = jnp.ones((SUBLANES, L), BF16)
        nt = (((1,), (1,)), ((), ()))
        q_dot_n = lax.dot_general(qb, n_rows8, nt, preferred_element_type=F32)[:, 0:1]
        s_sum = lax.dot_general(sb, ones8, nt, preferred_element_type=F32)[:, 0:1]
        den = w_inter * q_dot_n + s_sum
        h = num / jnp.maximum(jnp.abs(den), jnp.exp(-m_t))
        if update_state:
            g_row = total - b_row + ig_r
            g_col = total - b_col + ig_c
            m_new = jnp.maximum(total + m_prev, jnp.max(g_row, axis=1, keepdims=True))
            decay = jnp.exp(total + m_prev - m_new)
            kw = k * jnp.exp(g_col - m_new)
            c_scr[direction] = decay * c_prev + lax.dot_general(
                kw.astype(BF16), vb, (((0,), (0,)), ((), ())), preferred_element_type=F32)
            n_scr[direction] = decay * n_prev + jnp.sum(kw, axis=0, keepdims=True)
            m_scr[direction] = jnp.broadcast_to(m_new, (1, LANES))
        return h

    for direction in (0, 1):
        c_scr[direction] = c0_ref[direction]
        n_scr[direction] = n0_ref[direction]
        m_scr[direction] = m0_ref[direction]

    def both_directions(step, carry):
        update = with_state or n_chunks > 1
        rows_f = pl.ds(pl.multiple_of(step * L, L), L)
        rows_b = pl.ds(pl.multiple_of((n_chunks - 1 - step) * L, L), L)
        hf_scr[rows_f, :] = chunk_step(rows_f, 0, update)
        hb_scr[rows_b, :] = chunk_step(rows_b, 1, update)
        return carry
    lax.fori_loop(0, n_chunks, both_directions, 0)

    def finish(c, carry):
        rows = pl.ds(pl.multiple_of(c * L, L), L)
        hm = hf_scr[rows, :] + hb_scr[rows, :]
        mu = jnp.mean(hm, axis=1, keepdims=True)
        hc = hm - mu
        var = jnp.mean(hc * hc, axis=1, keepdims=True)
        hn = hc * lax.rsqrt(var + LN_EPS) * nw_ref[...]
        y_ref[rows, :] = (_sigmoid(og_ref[rows, :]) * hn).astype(y_ref.dtype)
        return carry
    lax.fori_loop(0, n_chunks, finish, 0)

    if with_state:
        for direction in (0, 1):
            c_out[direction] = c_scr[direction]
            n_out[direction] = n_scr[direction]
            m_out[direction] = m_scr[direction]


def _mlstm(proj, row_block0, n_seq, seq_len, g_cols, g_rows, c0, n0, m0, norm_w, with_state):
    hd = ML_HD
    body = functools.partial(_mlstm_body, seq_len=seq_len, with_state=with_state)
    bcast = c0.shape[0] == 1
    smap = (lambda s: 0) if bcast else (lambda s: s)
    col = lambda base: (lambda s, h: (row_block0 + s, base // hd + h))
    out_shape = [jax.ShapeDtypeStruct((n_seq * seq_len, D_ML), BF16)]
    out_specs = [pl.BlockSpec((seq_len, hd), lambda s, h: (s, h))]
    if with_state:
        out_shape += [jax.ShapeDtypeStruct((n_seq, 2, ML_HEADS, hd, hd), F32),
                      jax.ShapeDtypeStruct((n_seq, 2, ML_HEADS, 1, hd), F32),
                      jax.ShapeDtypeStruct((n_seq, 2, ML_HEADS, 1, LANES), F32)]
        out_specs += [pl.BlockSpec((None, 2, None, hd, hd), lambda s, h: (s, 0, h, 0, 0)),
                      pl.BlockSpec((None, 2, None, 1, hd), lambda s, h: (s, 0, h, 0, 0)),
                      pl.BlockSpec((None, 2, None, 1, LANES), lambda s, h: (s, 0, h, 0, 0))]
    return pl.pallas_call(
        body,
        out_shape=tuple(out_shape),
        grid=(n_seq, ML_HEADS),
        in_specs=[
            pl.BlockSpec((seq_len, hd), col(COL_Q)),
            pl.BlockSpec((seq_len, hd), col(COL_K)),
            pl.BlockSpec((seq_len, hd), col(COL_V)),
            pl.BlockSpec((seq_len, hd), col(COL_O)),
            pl.BlockSpec((None, seq_len, 4), lambda s, h: (h, row_block0 + s, 0)),
            pl.BlockSpec((None, 4, seq_len), lambda s, h: (h, 0, row_block0 + s)),
            pl.BlockSpec((None, 2, None, hd, hd), lambda s, h: (smap(s), 0, h, 0, 0)),
            pl.BlockSpec((None, 2, None, 1, hd), lambda s, h: (smap(s), 0, h, 0, 0)),
            pl.BlockSpec((None, 2, None, 1, LANES), lambda s, h: (smap(s), 0, h, 0, 0)),
            pl.BlockSpec((1, hd), lambda s, h: (0, h)),
        ],
        out_specs=tuple(out_specs),
        scratch_shapes=[pltpu.VMEM((seq_len, hd), F32), pltpu.VMEM((seq_len, hd), F32),
                        pltpu.VMEM((2, hd, hd), F32), pltpu.VMEM((2, 1, hd), F32), pltpu.VMEM((2, 1, LANES), F32)],
        compiler_params=_params("arbitrary", "arbitrary"),
        name="mlstm",
    )(proj, proj, proj, proj, g_cols, g_rows, c0, n0, m0, norm_w)


def _outproj_body(rg_ref, ml_ref, w_ref, x_ref, mod_ref, lng_ref, lnb_ref, rw_ref, rb_ref,
                  x1_ref, u2p_ref, ti_ref, tg_ref, *, alpha, n_experts, n_sub):
    tm = x_ref.shape[0]
    ts = tm // n_sub
    for sub in range(n_sub):
        rows = slice(sub * ts, (sub + 1) * ts)
        mix = (jnp.dot(rg_ref[rows, :], w_ref[0:D_RG, :], preferred_element_type=F32)
               + jnp.dot(ml_ref[rows, :], w_ref[D_RG:D_MODEL, :], preferred_element_type=F32))
        z = alpha * x_ref[rows, :] + mod_ref[2:3, :] * mix
        x1 = _layer_norm(z, lng_ref[...], lnb_ref[...])
        x1_ref[rows, :] = x1
        u2 = x1 * (1.0 + mod_ref[4:5, :]) + mod_ref[3:4, :]
        for s in range(TOKEN_TILE_ROWS):
            lo = u2[:, s * LANES:(s + 1) * LANES]
            hi = u2[:, D_MODEL // 2 + s * LANES:D_MODEL // 2 + (s + 1) * LANES]
            u2p_ref[pl.ds(sub * ts * TOKEN_TILE_ROWS + s, ts, stride=TOKEN_TILE_ROWS), :] = _pack_bf16_pair(lo, hi)
        logits = jnp.dot(u2, rw_ref[...], preferred_element_type=F32, precision=HIGHEST) + rb_ref[...]
        lane = lax.broadcasted_iota(jnp.int32, logits.shape, 1)
        logits = jnp.where(lane < n_experts, logits, -jnp.inf)
        vals, idxs = [], []
        for _ in range(TOP_K):
            m = jnp.max(logits, axis=1, keepdims=True)
            idx = jnp.min(jnp.where(logits == m, lane, LANES), axis=1, keepdims=True)
            vals.append(m)
            idxs.append(idx)
            logits = jnp.where(lane == idx, -jnp.inf, logits)
        exps = [jnp.exp(v - vals[0]) for v in vals]
        inv = 1.0 / (exps[0] + exps[1] + exps[2] + exps[3])
        ti = jnp.zeros(lane.shape, jnp.int32)
        tg = jnp.zeros(lane.shape, F32)
        for k in range(TOP_K):
            ti = jnp.where(lane == k, idxs[k], ti)
            tg = jnp.where(lane == k, exps[k] * inv, tg)
        ti_ref[rows, :] = ti
        tg_ref[rows, :] = tg


def _outproj(rg_y, ml_y, w_out_bf, x, mod_l, ln_g, ln_b, rw_pad, rb_pad, alpha, n_experts, n_p, ds, tm=512):
    n, d = x.shape
    body = functools.partial(_outproj_body, alpha=alpha, n_experts=n_experts, n_sub=OUTPROJ_SUBTILES)
    row = lambda i: (i, 0)
    const = lambda i: (0, 0)
    return pl.pallas_call(
        body,
        out_shape=(jax.ShapeDtypeStruct((n, d), F32),
                   jax.ShapeDtypeStruct((n * TOKEN_TILE_ROWS, LANES), jnp.uint32),
                   jax.ShapeDtypeStruct((n, LANES), jnp.int32), jax.ShapeDtypeStruct((n, LANES), F32)),
        grid=(n // tm,),
        in_specs=[
            pl.BlockSpec((tm, D_RG), row),
            pl.BlockSpec((tm, D_ML), row),
            pl.BlockSpec((d, d), const),
            pl.BlockSpec((tm, d), row),
            pl.BlockSpec((None, N_MOD, d), lambda i: (_cond_index(i * tm, n_p, ds), 0, 0)),
            pl.BlockSpec((1, d), const),
            pl.BlockSpec((1, d), const),
            pl.BlockSpec((d, LANES), const),
            pl.BlockSpec((1, LANES), const),
        ],
        out_specs=(pl.BlockSpec((tm, d), row), pl.BlockSpec((tm * TOKEN_TILE_ROWS, LANES), row),
                   pl.BlockSpec((tm, LANES), row), pl.BlockSpec((tm, LANES), row)),
        compiler_params=_params("arbitrary"),
        name="outproj",
    )(rg_y, ml_y, w_out_bf, x, mod_l, ln_g, ln_b, rw_pad, rb_pad)


def _moe_items(top_i, n_experts, max_items):
    n_assign = top_i.size
    e_flat = top_i.reshape(n_assign)
    order = jnp.argsort(e_flat, stable=True).astype(jnp.int32)
    counts = jnp.sum((e_flat[:, None] == jnp.arange(n_experts, dtype=jnp.int32)[None, :]).astype(jnp.int32), axis=0)
    starts = jnp.cumsum(counts) - counts
    items_per_e = (counts + MOE_CAP - 1) // MOE_CAP
    item_ends = jnp.cumsum(items_per_e)
    total = item_ends[-1]
    slot = jnp.arange(max_items, dtype=jnp.int32)
    valid = slot < total
    slot_c = jnp.minimum(slot, total - 1)
    e_s = jnp.minimum(jnp.sum((slot_c[:, None] >= item_ends[None, :]).astype(jnp.int32), axis=1), n_experts - 1)
    local = slot_c - (item_ends[e_s] - items_per_e[e_s])
    start = starts[e_s] + local * MOE_CAP
    n_rows = jnp.clip(counts[e_s] - local * MOE_CAP, 0, MOE_CAP)
    return (order, e_s, jnp.where(valid, start, 0).astype(jnp.int32),
            jnp.where(valid, n_rows, 0).astype(jnp.int32), total.reshape(1).astype(jnp.int32))


def _moe_body(src_row_ref, dst_row_ref, item_e_ref, item_start_ref, item_n_ref, n_items_ref,
              u2p_hbm, w1_hbm, w2_hbm, b1_ref, b2_ref,
              y4p_hbm,
              xq_scr, xb_scr, acc_scr, ys_scr, w1g_f, w1l_f, w2_f, w1g_b, w1l_b, w2_b, gsem, ssem, wsem,
              *, layer, n_assign, n_tokens, n_ft):
    d = acc_scr.shape[1]
    half = d // 2
    de = n_ft * MOE_TF
    tr = TOKEN_TILE_ROWS
    n_items = n_items_ref[0]
    n_steps = n_items * n_ft

    def tile(ref, idx):
        return ref.at[pl.ds(pl.multiple_of(idx * tr, tr), tr)]

    def rows8(ref, row):
        return ref.at[pl.ds(pl.multiple_of(row, tr), tr)]

    def granules(n_rows):
        return (n_rows + MOE_RB - 1) // MOE_RB

    def gather_quantum(it, k, lo=0, hi=MOE_QUANTUM):
        base = item_start_ref[it] + k * MOE_QUANTUM
        for i in range(lo, hi):
            src = src_row_ref[jnp.minimum(base + i, n_assign - 1)]
            pltpu.make_async_copy(rows8(u2p_hbm, src), tile(xq_scr, k * MOE_QUANTUM + i),
                                  gsem).start(priority=i % 2)

    def scatter_quantum(it, n_valid, k, lo=0, hi=MOE_QUANTUM):
        base = item_start_ref[it] + k * MOE_QUANTUM
        for i in range(lo, hi):
            r = k * MOE_QUANTUM + i
            real = dst_row_ref[jnp.minimum(base + i, n_assign - 1)]
            dst = jnp.where(r < n_valid, real, (TOP_K * n_tokens + r) * tr)
            pltpu.make_async_copy(tile(ys_scr, r), rows8(y4p_hbm, dst), ssem).start(priority=i % 2)

    def quanta(lo, hi, fn):
        def f(k, c):
            fn(k)
            return c
        lax.fori_loop(lo, hi, f, 0)

    def wait_tiles(count, src, dst, sem):
        def group(g, c):
            for _ in range(MOE_DMA_UNROLL):
                pltpu.make_async_copy(tile(src, 0), tile(dst, 0), sem).wait()
            return c
        lax.fori_loop(0, count // MOE_DMA_UNROLL, group, 0)

    def weight_copies(step):
        it = step // n_ft
        j = step - it * n_ft
        e = item_e_ref[it]
        slot = lax.rem(step, MOE_W_SLOTS)
        col = pl.multiple_of(j * MOE_TF, MOE_TF)
        col_l = pl.multiple_of(de + j * MOE_TF, MOE_TF)
        return (pltpu.make_async_copy(w1_hbm.at[layer, e, :, pl.ds(col, MOE_TF)], w1g_f.at[slot], wsem.at[slot]),
                pltpu.make_async_copy(w1_hbm.at[layer, e, :, pl.ds(col_l, MOE_TF)], w1l_f.at[slot], wsem.at[slot]),
                pltpu.make_async_copy(w2_hbm.at[layer, e, pl.ds(col, MOE_TF), :], w2_f.at[slot], wsem.at[slot]))

    def cast_part(part, src, dst):
        h = MOE_TF // 2
        if part == 0:
            w1g_b[dst] = w1g_f[src].astype(BF16)
            w2_b[dst, 0:h, :] = w2_f[src, 0:h, :].astype(BF16)
        else:
            w1l_b[dst] = w1l_f[src].astype(BF16)
            w2_b[dst, h:MOE_TF, :] = w2_f[src, h:MOE_TF, :].astype(BF16)

    def compute(row0, m, wb, b1g, b1l):
        rows = pl.ds(row0, m)
        x = xb_scr[rows, :]
        hg = jnp.dot(x, w1g_b[wb], preferred_element_type=F32) + b1g
        hl = jnp.dot(x, w1l_b[wb], preferred_element_type=F32) + b1l
        hg = jnp.minimum(hg, SWIGLU_LIMIT)
        hl = jnp.clip(hl, -SWIGLU_LIMIT, SWIGLU_LIMIT)
        act = hg * _sigmoid(SWIGLU_ALPHA * hg) * (hl + 1.0)
        acc_scr[rows, :] += jnp.dot(act.astype(BF16), w2_b[wb], preferred_element_type=F32)

    for cp in weight_copies(0):
        cp.start()
    for cp in weight_copies(1):
        cp.start()

    def zero_ys(b, c):
        ys_scr[pl.ds(pl.multiple_of(b * (MOE_RB * tr), MOE_RB * tr), MOE_RB * tr), :] = jnp.zeros(
            (MOE_RB * tr, LANES), jnp.uint32)
        return c
    lax.fori_loop(0, MOE_CAP // MOE_RB, zero_ys, 0)
    spare = pltpu.make_async_copy(ys_scr, y4p_hbm.at[pl.ds(TOP_K * n_tokens * tr, MOE_CAP * tr)], ssem)
    spare.start()
    spare.wait()
    for cp in weight_copies(0):
        cp.wait()
    cast_part(0, 0, 0)
    cast_part(1, 0, 0)

    def item_body(it, carry):
        n_prev, g_issued = carry
        it_prev = jnp.maximum(it - 1, 0)
        it_next = jnp.minimum(it + 1, n_items - 1)
        n_rows = item_n_ref[it]
        e = item_e_ref[it]
        n_g = granules(n_rows)
        n_quads = n_g // 4
        q_per = jnp.minimum(n_quads, MOE_SIDE_QUANTA // n_ft)
        quanta(g_issued, MOE_SIDE_QUANTA, lambda k: gather_quantum(it, k))
        wait_tiles(MOE_CAP, u2p_hbm, xq_scr, gsem)

        def unpack(b, c):
            rows = pl.ds(pl.multiple_of(b * MOE_RB, MOE_RB), MOE_RB)
            for t in range(tr):
                lo, hi = _unpack_bf16_pair(xq_scr[pl.ds(b * (MOE_RB * tr) + t, MOE_RB, stride=tr), :])
                xb_scr[rows, t * LANES:(t + 1) * LANES] = lo.astype(BF16)
                xb_scr[rows, half + t * LANES:half + (t + 1) * LANES] = hi.astype(BF16)
            acc_scr[rows, :] = jnp.broadcast_to(b2_ref[pl.ds(e, 1), :], (MOE_RB, d))
            return c
        lax.fori_loop(0, n_g, unpack, 0)

        def tile_step(j, c):
            step = it * n_ft + j
            wb = step & 1
            nxt = lax.rem(step + 1, MOE_W_SLOTS)

            @pl.when(step + 2 < n_steps)
            def _prefetch_weights():
                for cp in weight_copies(step + 2):
                    cp.start()

            @pl.when(step + 1 < n_steps)
            def _next_weights_ready():
                for cp in weight_copies(step + 1):
                    cp.wait()

            b1g = b1_ref[e, pl.ds(j, 1), :]
            b1l = b1_ref[e, pl.ds(n_ft + j, 1), :]
            @pl.when(n_quads >= 2)
            def _eight_granules():
                compute(0, 8 * MOE_RB, wb, b1g, b1l)
                cast_part(0, nxt, 1 - wb)
                cast_part(1, nxt, 1 - wb)
                for k in (2 * j, 2 * j + 1):
                    scatter_quantum(it_prev, n_prev, k)
                    gather_quantum(it_next, k)

            @pl.when(n_quads == 1)
            def _four_granules():
                compute(0, 4 * MOE_RB, wb, b1g, b1l)
                cast_part(0, nxt, 1 - wb)
                cast_part(1, nxt, 1 - wb)
                scatter_quantum(it_prev, n_prev, j)
                gather_quantum(it_next, j)

            @pl.when(n_quads == 0)
            def _no_big_chunk():
                cast_part(0, nxt, 1 - wb)
                cast_part(1, nxt, 1 - wb)

            @pl.when((n_g & 2) != 0)
            def _two():
                compute(pl.multiple_of(n_quads * (4 * MOE_RB), 2 * MOE_RB), 2 * MOE_RB, wb, b1g, b1l)

            @pl.when((n_g & 1) != 0)
            def _one():
                compute(pl.multiple_of((n_g - 1) * MOE_RB, MOE_RB), MOE_RB, wb, b1g, b1l)
            return c
        lax.fori_loop(0, n_ft, tile_step, 0)

        issued = n_ft * q_per
        quanta(issued, MOE_SIDE_QUANTA, lambda k: scatter_quantum(it_prev, n_prev, k))
        wait_tiles(MOE_CAP, ys_scr, y4p_hbm, ssem)

        def pack(b, c):
            rows = pl.ds(pl.multiple_of(b * MOE_RB, MOE_RB), MOE_RB)
            for t in range(tr):
                lo = acc_scr[rows, t * LANES:(t + 1) * LANES]
                hi = acc_scr[rows, half + t * LANES:half + (t + 1) * LANES]
                ys_scr[pl.ds(b * (MOE_RB * tr) + t, MOE_RB, stride=tr), :] = _pack_bf16_pair(lo, hi)
            return c
        lax.fori_loop(0, n_g, pack, 0)
        return n_rows, issued

    n_last, g_extra = lax.fori_loop(0, n_items, item_body, (jnp.int32(0), jnp.int32(0)))
    wait_tiles(g_extra * MOE_QUANTUM, u2p_hbm, xq_scr, gsem)
    quanta(0, MOE_SIDE_QUANTA, lambda k: scatter_quantum(n_items - 1, n_last, k))
    wait_tiles(MOE_CAP, ys_scr, y4p_hbm, ssem)


def _moe(u2p, top_i, layer, w1, b1, w2, b2):
    n = u2p.shape[0] // TOKEN_TILE_ROWS
    depth, n_experts, d, two_de = w1.shape
    de = two_de // 2
    n_ft = de // MOE_TF
    n_assign = n * TOP_K
    max_items = n_experts + n_assign // MOE_CAP
    order, item_e, item_start, item_n, n_items = _moe_items(top_i, n_experts, max_items)
    src_row = (order // TOP_K) * TOKEN_TILE_ROWS
    dst_row = ((order % TOP_K) * n + order // TOP_K) * TOKEN_TILE_ROWS
    assert MOE_SIDE_QUANTA % n_ft == 0 and MOE_SIDE_QUANTA // n_ft >= MOE_CAP // (4 * MOE_RB)
    body = functools.partial(_moe_body, layer=layer, n_assign=n_assign, n_tokens=n, n_ft=n_ft)
    tiles = MOE_CAP * TOKEN_TILE_ROWS
    grid_spec = pltpu.PrefetchScalarGridSpec(
        num_scalar_prefetch=6,
        grid=(1,),
        in_specs=[
            pl.BlockSpec(memory_space=pl.ANY),
            pl.BlockSpec(memory_space=pl.ANY),
            pl.BlockSpec(memory_space=pl.ANY),
            pl.BlockSpec((None, n_experts, 2 * n_ft, MOE_TF), lambda i, *_: (layer, 0, 0, 0)),
            pl.BlockSpec((None, n_experts, d), lambda i, *_: (layer, 0, 0)),
        ],
        out_specs=pl.BlockSpec(memory_space=pl.ANY),
        scratch_shapes=[
            pltpu.VMEM((tiles, LANES), jnp.uint32), pltpu.VMEM((MOE_CAP, d), BF16),
            pltpu.VMEM((MOE_CAP, d), F32), pltpu.VMEM((tiles, LANES), jnp.uint32),
            pltpu.VMEM((MOE_W_SLOTS, d, MOE_TF), F32), pltpu.VMEM((MOE_W_SLOTS, d, MOE_TF), F32),
            pltpu.VMEM((MOE_W_SLOTS, MOE_TF, d), F32),
            pltpu.VMEM((2, d, MOE_TF), BF16), pltpu.VMEM((2, d, MOE_TF), BF16), pltpu.VMEM((2, MOE_TF, d), BF16),
            pltpu.SemaphoreType.DMA, pltpu.SemaphoreType.DMA, pltpu.SemaphoreType.DMA((MOE_W_SLOTS,)),
        ],
    )
    return pl.pallas_call(
        body,
        out_shape=jax.ShapeDtypeStruct(((n_assign + MOE_CAP) * TOKEN_TILE_ROWS, LANES), jnp.uint32),
        grid_spec=grid_spec,
        compiler_params=_params("arbitrary"),
        name="moe",
    )(src_row, dst_row, item_e, item_start, item_n, n_items, u2p, w1, w2,
      b1.reshape(depth, n_experts, 2 * n_ft, MOE_TF), b2)


def _combine_body(x1_ref, y0_ref, y1_ref, y2_ref, y3_ref, tg_ref, mod_ref, modn_ref, lng_ref, lnb_ref,
                  x2_ref, un_ref, ff_scr, *, alpha):
    tm, d = x1_ref.shape
    half = d // 2
    tr = TOKEN_TILE_ROWS
    tg = tg_ref[...]
    for t in range(tr):
        lo_sum = hi_sum = None
        for k, y_ref in enumerate((y0_ref, y1_ref, y2_ref, y3_ref)):
            lo, hi = _unpack_bf16_pair(y_ref[pl.ds(t, tm, stride=tr), :])
            g = tg[:, k:k + 1]
            lo_sum = g * lo if lo_sum is None else lo_sum + g * lo
            hi_sum = g * hi if hi_sum is None else hi_sum + g * hi
        ff_scr[:, t * LANES:(t + 1) * LANES] = lo_sum
        ff_scr[:, half + t * LANES:half + (t + 1) * LANES] = hi_sum
    z = alpha * x1_ref[...] + mod_ref[5:6, :] * ff_scr[...]
    x2 = _layer_norm(z, lng_ref[...], lnb_ref[...])
    x2_ref[...] = x2
    un_ref[...] = (x2 * (1.0 + modn_ref[1:2, :]) + modn_ref[0:1, :]).astype(BF16)


def _combine(x1, y4p, tg, mod_l, mod_next, ln_g, ln_b, alpha, n_p, ds, tm=256):
    n, d = x1.shape
    body = functools.partial(_combine_body, alpha=alpha)
    row = lambda i: (i, 0)
    const = lambda i: (0, 0)
    cond = lambda i: (_cond_index(i * tm, n_p, ds), 0, 0)
    y_specs = [pl.BlockSpec((tm * TOKEN_TILE_ROWS, LANES), lambda i, k=k: (k * (n // tm) + i, 0))
               for k in range(TOP_K)]
    return pl.pallas_call(
        body,
        out_shape=(jax.ShapeDtypeStruct((n, d), F32), jax.ShapeDtypeStruct((n, d), BF16)),
        grid=(n // tm,),
        in_specs=[
            pl.BlockSpec((tm, d), row),
            *y_specs,
            pl.BlockSpec((tm, LANES), row),
            pl.BlockSpec((None, N_MOD, d), cond),
            pl.BlockSpec((None, N_MOD, d), cond),
            pl.BlockSpec((1, d), const),
            pl.BlockSpec((1, d), const),
        ],
        out_specs=(pl.BlockSpec((tm, d), row), pl.BlockSpec((tm, d), row)),
        scratch_shapes=[pltpu.VMEM((tm, d), F32)],
        compiler_params=_params("arbitrary"),
        name="combine",
    )(x1, y4p, y4p, y4p, y4p, tg, mod_l, mod_next, ln_g, ln_b)


def _block_diag_tiles(w, tc):
    depth, nb, bw, _ = w.shape
    per = tc // bw
    wt = w.reshape(depth, nb // per, per, bw, bw)
    on_diag = jnp.eye(per, dtype=bool)[None, None, :, None, :, None]
    full = jnp.where(on_diag, wt[:, :, :, :, None, :], jnp.zeros((), w.dtype))
    return full.reshape(depth, nb // per, tc, tc)


def _rg_gate_weights(rg_wa, rg_ba, rg_wx, rg_bx, tc=256):
    depth = rg_wa.shape[0]
    n_ct = D_RG // tc
    tiles = [_block_diag_tiles(w[:, d], tc) for d in (0, 1) for w in (rg_wa, rg_wx)]
    wg = jnp.concatenate(tiles, axis=-1).astype(BF16)
    biases = [b[:, d].reshape(depth, n_ct, 1, tc) for d in (0, 1) for b in (rg_ba, rg_bx)]
    bg = jnp.concatenate(biases, axis=-1)
    return wg, bg


def kernel(x_prompt, x_sample, state_rglru, state_mlstm_C, state_mlstm_n, state_mlstm_m, c, c_ctx, w_ada, b_ada, w_in, ml_gate_b, rg_conv_w, rg_conv_b, rg_wa, rg_ba, rg_wx, rg_bx, rg_lambda, ml_norm_w, w_out, ln1_g, ln1_b, router_w, router_b, moe_w1, moe_b1, moe_w2, moe_b2, ln2_g, ln2_b):
    depth = w_ada.shape[0]
    bsz, seq, d = x_prompt.shape
    dbs, dseq, _ = x_sample.shape
    n_p, n_s = bsz * seq, dbs * dseq
    n_experts = router_w.shape[-1]
    alpha = float((2 * depth) ** 0.25)
    assert d == D_MODEL and seq == CHUNK and dseq % CHUNK == 0 and n_p % dseq == 0

    x = jnp.concatenate([x_prompt.reshape(n_p, d), x_sample.reshape(n_s, d)], axis=0)
    n_cond = 1 + dbs
    conds = jnp.concatenate([c_ctx[None, :], c, jnp.zeros((SUBLANES - n_cond, d), F32)], axis=0)
    mod = _ada(conds, w_ada, b_ada)
    mod = mod[:, :n_cond].reshape(depth, n_cond, N_MOD, d)

    w_in_bf = w_in.astype(BF16)
    w_g = w_in[:, :, D_PROJ:]
    wg_pad = jnp.pad(w_g, ((0, 0), (0, 0), (0, LANES - N_GATES))).astype(BF16)
    wgt = jnp.swapaxes(w_g, 1, 2).astype(BF16)
    gb_row = jnp.pad(ml_gate_b, ((0, 0), (0, LANES - N_GATES))).reshape(depth, 1, LANES)
    gb_col = ml_gate_b.reshape(depth, N_GATES, 1)
    rg_wg, rg_bg = _rg_gate_weights(rg_wa, rg_ba, rg_wx, rg_bx)
    w_out_bf = w_out.astype(BF16)
    rw_pad = jnp.pad(router_w, ((0, 0), (0, 0), (0, LANES - n_experts)))
    rb_pad = jnp.pad(router_b, ((0, 0), (0, LANES - n_experts))).reshape(depth, 1, LANES)

    zero_rg = jnp.zeros((1, 2, D_RG), F32)
    zero_c = jnp.zeros((1, 2, ML_HEADS, ML_HD, ML_HD), F32)
    zero_n = jnp.zeros((1, 2, ML_HEADS, 1, ML_HD), F32)
    zero_m = jnp.zeros((1, 2, ML_HEADS, 1, LANES), F32)

    u = _modulate(x, mod[0], n_p, dseq)
    states = []
    for l in range(depth):
        proj = _inproj(u, w_in_bf[l])
        g, gt = _gates(u, wg_pad[l], wgt[l], gb_row[l], gb_col[l])
        g_cols = g[:, :N_GATES].reshape(n_p + n_s, 4, ML_HEADS).transpose(2, 0, 1)
        g_rows = gt.reshape(4, ML_HEADS, n_p + n_s).transpose(1, 0, 2)

        conv_b = rg_conv_b[l].reshape(1, D_RG)
        rg_p, ht_p = _rglru(proj, 0, bsz, seq, seq, rg_conv_w[l], conv_b, rg_wg[l], rg_bg[l], rg_lambda[l], zero_rg)
        rg_s, _ = _rglru(proj, n_p // dseq, dbs, dseq, GRID_W, rg_conv_w[l], conv_b, rg_wg[l], rg_bg[l],
                         rg_lambda[l], state_rglru[:, l])

        norm_w = ml_norm_w[l].reshape(1, D_ML)
        ml_p, c_p, nn_p, m_p = _mlstm(proj, 0, bsz, seq, g_cols, g_rows, zero_c, zero_n, zero_m, norm_w, True)
        m0_s = jnp.broadcast_to(state_mlstm_m[:, l][..., None, None], (dbs, 2, ML_HEADS, 1, LANES))
        (ml_s,) = _mlstm(proj, n_p // dseq, dbs, dseq, g_cols, g_rows, state_mlstm_C[:, l],
                         state_mlstm_n[:, l][:, :, :, None, :], m0_s, norm_w, False)
        states.append((ht_p, c_p, nn_p[:, :, :, 0, :], m_p[:, :, :, 0, 0]))

        rg_y = jnp.concatenate([rg_p, rg_s], axis=0)
        ml_y = jnp.concatenate([ml_p, ml_s], axis=0)
        x1, u2p, ti, tg = _outproj(rg_y, ml_y, w_out_bf[l], x, mod[l], ln1_g[l].reshape(1, d), ln1_b[l].reshape(1, d),
                                  rw_pad[l], rb_pad[l], alpha, n_experts, n_p, dseq)
        y4p = _moe(u2p, ti[:, :TOP_K], l, moe_w1, moe_b1, moe_w2, moe_b2)
        x, u = _combine(x1, y4p, tg, mod[l], mod[min(l + 1, depth - 1)], ln2_g[l].reshape(1, d),
                        ln2_b[l].reshape(1, d), alpha, n_p, dseq)

    new_rglru = jnp.stack([s[0] for s in states], axis=1)
    new_c = jnp.stack([s[1] for s in states], axis=1)
    new_n = jnp.stack([s[2] for s in states], axis=1)
    new_m = jnp.stack([s[3] for s in states], axis=1)
    return (x[:n_p].reshape(bsz, seq, d), x[n_p:].reshape(dbs, dseq, d), new_rglru, new_c, new_n, new_m)
```
